```python
import jax
import jax.numpy as jnp
from jax import lax
import numpy as np

D_MODEL = 4096
BATCH = 2
SEQ = 4096
DEPTH = 1
DEC_BATCH = 128
DEC_SEQ = 4
PAST_LEN = 2048
PAGE_SIZE = 128

HEAD_DIM = 128
D_MIX = D_MODEL
N_ATT_HEADS = (3 * D_MIX // 4) // HEAD_DIM
D_ATT = N_ATT_HEADS * HEAD_DIM
D_CONV = D_MIX - D_ATT
CONV_WIDTH = 3
DILATED_BRANCHES = ((128, 1), (512, 4), (2048, 16))
MAX_WINDOW = max(w for w, _ in DILATED_BRANCHES)
D_IN = 3 * D_ATT + 3 * D_CONV
N_EXPERTS = 32
TOP_K = 4
D_FF = D_MODEL
SWIGLU_ALPHA = 1.702
SWIGLU_LIMIT = 7.0
EXPERT_BLOCK = 128
RMS_EPS = 1e-5

kernel_name = "hymba_dilated_shortconv_moe_step"


def rmsnorm(x, g):
    xf = x.astype(jnp.float32)
    xf = xf * lax.rsqrt(jnp.mean(xf * xf, axis=-1, keepdims=True) + RMS_EPS)
    return xf.astype(x.dtype) * g


def split_projection(xn, w_in):
    proj = xn @ w_in
    cuts = [D_ATT, 2 * D_ATT, 3 * D_ATT, 3 * D_ATT + D_CONV, 3 * D_ATT + 2 * D_CONV]
    q, k, v, gate_b, gate_c, conv_in = jnp.split(proj, cuts, axis=-1)
    to_heads = lambda t: t.reshape(t.shape[:-1] + (N_ATT_HEADS, HEAD_DIM))
    return to_heads(q), to_heads(k), to_heads(v), gate_b, gate_c * conv_in


def causal_conv(u_ext, w):
    n_out = u_ext.shape[1] - (CONV_WIDTH - 1)
    out = w[0] * u_ext[:, :n_out]
    for tap in range(1, CONV_WIDTH):
        out = out + w[tap] * u_ext[:, tap:tap + n_out]
    return out


def masked_softmax_stats(s, mask):
    s = jnp.where(mask, s, -jnp.inf)
    m = jnp.max(s, axis=-1)
    p = jnp.exp(s - m[..., None])
    return p, m, jnp.sum(p, axis=-1)


def dilated_prompt(q, k, v, window, dil):
    B, S, H, Dh = q.shape
    steps = window // dil
    span = steps * dil
    nb = -(-S // span)
    pad = nb * span - S

    def blocks(t):
        return jnp.pad(t, ((0, 0), (0, pad), (0, 0), (0, 0))).reshape(B, nb, steps, dil, H, Dh)

    def with_prev(t):
        prev = jnp.concatenate([jnp.zeros_like(t[:, :1]), t[:, :-1]], axis=1)
        return jnp.concatenate([prev, t], axis=2)

    qb = blocks(q)
    kc = with_prev(blocks(k))
    vc = with_prev(blocks(v))
    s = jnp.einsum('bnqrhd,bnkrhd->bnrhqk', qb, kc).astype(jnp.float32) * (HEAD_DIM ** -0.5)
    qi = jnp.arange(steps)[:, None]
    kj = jnp.arange(2 * steps)[None, :]
    dist = qi + steps - kj
    blk = jnp.arange(nb)[:, None, None]
    mask = (dist >= 0) & (dist <= steps) & ((blk > 0) | (kj >= steps))
    p, m, l = masked_softmax_stats(s, mask[None, :, None, None])
    acc = jnp.einsum('bnrhqk,bnkrhd->bnqrhd', p, vc.astype(jnp.float32))
    acc = acc.reshape(B, nb * span, H, Dh)[:, :S]
    m = m.transpose(0, 1, 4, 2, 3).reshape(B, nb * span, H)[:, :S]
    l = l.transpose(0, 1, 4, 2, 3).reshape(B, nb * span, H)[:, :S]
    return acc, m, l


def dilated_sample(q_new, k_all, v_all, n_past, window, dil):
    T = q_new.shape[1]
    steps = window // dil
    idx = n_past + jnp.arange(T)[:, None] - dil * jnp.arange(steps + 1)[None, :]
    valid = idx >= 0
    idx = jnp.maximum(idx, 0)
    kg = k_all[:, idx]
    vg = v_all[:, idx]
    s = jnp.einsum('bthd,btjhd->bhtj', q_new, kg).astype(jnp.float32) * (HEAD_DIM ** -0.5)
    p, m, l = masked_softmax_stats(s, valid[None, None])
    acc = jnp.einsum('bhtj,btjhd->bthd', p, vg.astype(jnp.float32))
    return acc, m.transpose(0, 2, 1), l.transpose(0, 2, 1)


def combine_branches(parts):
    m_all = jnp.stack([m for _, m, _ in parts])
    w = jnp.exp(m_all - jnp.max(m_all, axis=0))
    num = sum(w[i][..., None] * acc for i, (acc, _, _) in enumerate(parts))
    den = sum(w[i] * l for i, (_, _, l) in enumerate(parts))
    return num / den[..., None]


def mixer_output(attn, conv, g_attn, g_conv, w_out, dtype):
    attn = attn.reshape(attn.shape[:-2] + (D_ATT,)).astype(dtype)
    cat = jnp.concatenate([rmsnorm(attn, g_attn), rmsnorm(conv, g_conv)], axis=-1)
    return cat @ w_out


def moe_ffn(x, w_router, b_router, w_gate_up, b_gate_up, w_down, b_down, layer):
    n, d = x.shape
    logits = (x @ w_router[layer] + b_router[layer]).astype(jnp.float32)
    top_logit, top_e = lax.top_k(logits, TOP_K)
    gate = jax.nn.softmax(top_logit, axis=-1)
    n_assign = n * TOP_K
    flat_e = top_e.reshape(-1)
    flat_tok = jnp.arange(n_assign, dtype=jnp.int32) // TOP_K
    flat_gate = gate.reshape(-1).astype(x.dtype)
    order = jnp.argsort(flat_e)
    e_sorted = flat_e[order]
    counts = jnp.bincount(flat_e, length=N_EXPERTS)
    padded = (counts + EXPERT_BLOCK - 1) // EXPERT_BLOCK * EXPERT_BLOCK
    pad_end = jnp.cumsum(padded)
    pad_start = pad_end - padded
    start = jnp.cumsum(counts) - counts
    dest = pad_start[e_sorted] + jnp.arange(n_assign) - start[e_sorted]
    n_blocks = -(-n_assign // EXPERT_BLOCK) + N_EXPERTS
    n_slots = n_blocks * EXPERT_BLOCK
    slot_tok = jnp.full((n_slots,), n, jnp.int32).at[dest].set(flat_tok[order])
    slot_gate = jnp.zeros((n_slots,), x.dtype).at[dest].set(flat_gate[order])
    block_e = jnp.minimum(
        jnp.searchsorted(pad_end, jnp.arange(n_blocks) * EXPERT_BLOCK, side='right'), N_EXPERTS - 1)
    x_ext = jnp.concatenate([x, jnp.zeros((1, d), x.dtype)], axis=0)

    def expert_block(args):
        tok, e = args
        gu = x_ext[tok] @ w_gate_up[layer, e] + b_gate_up[layer, e]
        glu = jnp.minimum(gu[:, :D_FF], SWIGLU_LIMIT)
        lin = jnp.clip(gu[:, D_FF:], -SWIGLU_LIMIT, SWIGLU_LIMIT)
        h = glu * jax.nn.sigmoid(SWIGLU_ALPHA * glu) * (lin + 1.0)
        return h @ w_down[layer, e] + b_down[layer, e]

    y_slots = lax.map(expert_block, (slot_tok.reshape(n_blocks, EXPERT_BLOCK), block_e))
    y = jax.ops.segment_sum(y_slots.reshape(n_slots, d) * slot_gate[:, None], slot_tok,
                            num_segments=n + 1)
    return y[:n]


def setup_inputs(seed: int = 0) -> dict:
    key = jax.random.key(seed)
    ks = jax.random.split(key, 20)
    nrm = jax.random.normal
    w_buf = min(MAX_WINDOW, PAST_LEN)
    return {
        "x_prompt": nrm(ks[0], (BATCH, SEQ, D_MODEL), jnp.float32),
        "x_sample": nrm(ks[1], (DEC_BATCH, DEC_SEQ, D_MODEL), jnp.float32),
        "cache_k": nrm(ks[2], (DEPTH, DEC_BATCH, w_buf, N_ATT_HEADS, HEAD_DIM), jnp.float32),
        "cache_v": nrm(ks[3], (DEPTH, DEC_BATCH, w_buf, N_ATT_HEADS, HEAD_DIM), jnp.float32),
        "state_conv": nrm(ks[4], (DEPTH, DEC_BATCH, CONV_WIDTH - 1, D_CONV), jnp.float32),
        "norm_mix": 1.0 + 0.02 * nrm(ks[5], (DEPTH, D_MODEL), jnp.float32),
        "w_in": nrm(ks[6], (DEPTH, D_MODEL, D_IN), jnp.float32) * D_MODEL ** -0.5,
        "conv_w": nrm(ks[7], (DEPTH, CONV_WIDTH, D_CONV), jnp.float32) * CONV_WIDTH ** -0.5,
        "norm_attn_out": 1.0 + 0.02 * nrm(ks[8], (DEPTH, D_ATT), jnp.float32),
        "norm_conv_out": 1.0 + 0.02 * nrm(ks[9], (DEPTH, D_CONV), jnp.float32),
        "w_out": nrm(ks[10], (DEPTH, D_MIX, D_MODEL), jnp.float32) * D_MIX ** -0.5,
        "norm_ffn": 1.0 + 0.02 * nrm(ks[11], (DEPTH, D_MODEL), jnp.float32),
        "w_router": nrm(ks[12], (DEPTH, D_MODEL, N_EXPERTS), jnp.float32) * D_MODEL ** -0.5,
        "b_router": 0.01 * nrm(ks[13], (DEPTH, N_EXPERTS), jnp.float32),
        "w_gate_up": nrm(ks[14], (DEPTH, N_EXPERTS, D_MODEL, 2 * D_FF), jnp.float32) * D_MODEL ** -0.5,
        "b_gate_up": 0.01 * nrm(ks[15], (DEPTH, N_EXPERTS, 2 * D_FF), jnp.float32),
        "w_down": nrm(ks[16], (DEPTH, N_EXPERTS, D_FF, D_MODEL), jnp.float32) * D_FF ** -0.5,
        "b_down": 0.01 * nrm(ks[17], (DEPTH, N_EXPERTS, D_MODEL), jnp.float32),
        "norm_final": 1.0 + 0.02 * nrm(ks[18], (D_MODEL,), jnp.float32),
    }


def reference(x_prompt, x_sample, cache_k, cache_v, state_conv, norm_mix, w_in, conv_w,
              norm_attn_out, norm_conv_out, w_out, norm_ffn, w_router, b_router,
              w_gate_up, b_gate_up, w_down, b_down, norm_final):
    hp, hs = x_prompt, x_sample
    n_past = cache_k.shape[2]
    new_kp, new_vp, new_cp, new_ks, new_vs, new_cs = [], [], [], [], [], []
    for l in range(DEPTH):
        qp, kp, vp, bp, up = split_projection(rmsnorm(hp, norm_mix[l]), w_in[l])
        attn_p = combine_branches([dilated_prompt(qp, kp, vp, w, d) for (w, d) in DILATED_BRANCHES])
        up_ext = jnp.pad(up, ((0, 0), (CONV_WIDTH - 1, 0), (0, 0)))
        conv_p = bp * causal_conv(up_ext, conv_w[l])
        hp = hp + mixer_output(attn_p, conv_p, norm_attn_out[l], norm_conv_out[l], w_out[l], hp.dtype)

        qs, ks_, vs, bs, us = split_projection(rmsnorm(hs, norm_mix[l]), w_in[l])
        k_all = jnp.concatenate([cache_k[l], ks_], axis=1)
        v_all = jnp.concatenate([cache_v[l], vs], axis=1)
        attn_s = combine_branches([dilated_sample(qs, k_all, v_all, n_past, w, d)
                                   for (w, d) in DILATED_BRANCHES])
        us_ext = jnp.concatenate([state_conv[l], us], axis=1)
        conv_s = bs * causal_conv(us_ext, conv_w[l])
        hs = hs + mixer_output(attn_s, conv_s, norm_attn_out[l], norm_conv_out[l], w_out[l], hs.dtype)

        hp = hp + moe_ffn(rmsnorm(hp, norm_ffn[l]).reshape(-1, D_MODEL), w_router, b_router,
                          w_gate_up, b_gate_up, w_down, b_down, l).reshape(hp.shape)
        hs = hs + moe_ffn(rmsnorm(hs, norm_ffn[l]).reshape(-1, D_MODEL), w_router, b_router,
                          w_gate_up, b_gate_up, w_down, b_down, l).reshape(hs.shape)

        n_keep = min(MAX_WINDOW, hp.shape[1])
        new_kp.append(kp[:, -n_keep:])
        new_vp.append(vp[:, -n_keep:])
        new_cp.append(up[:, -(CONV_WIDTH - 1):])
        new_ks.append(ks_)
        new_vs.append(vs)
        new_cs.append(us_ext[:, -(CONV_WIDTH - 1):])
    y_prompt = rmsnorm(hp, norm_final)
    y_sample = rmsnorm(hs, norm_final)
    return (y_prompt, y_sample, jnp.stack(new_kp), jnp.stack(new_vp), jnp.stack(new_cp),
            jnp.stack(new_ks), jnp.stack(new_vs), jnp.stack(new_cs))
```

```python
import functools

import numpy as np
import jax
import jax.numpy as jnp
from jax import lax
from jax.experimental import pallas as pl
from jax.experimental.pallas import tpu as pltpu

D_MODEL = 4096
HEAD_DIM = 128
N_HEADS = 24
D_ATT = N_HEADS * HEAD_DIM
D_CONV = D_MODEL - D_ATT
CONV_WIDTH = 3
BRANCHES = ((128, 1), (512, 4), (2048, 16))
MAX_WINDOW = 2048
D_IN = 3 * D_ATT + 3 * D_CONV
N_EXPERTS = 32
TOP_K = 4
D_FF = D_MODEL
SWIGLU_ALPHA = 1.702
SWIGLU_LIMIT = 7.0
RMS_EPS = 1e-5

F32 = jnp.float32
BF16 = jnp.bfloat16

VMEM_LIMIT_BYTES = 56 * 1024 * 1024


def _params(semantics):
    return pltpu.CompilerParams(dimension_semantics=semantics,
                                vmem_limit_bytes=VMEM_LIMIT_BYTES)


def _branch_weight(dist):
    dist = np.asarray(dist)
    c = np.zeros(dist.shape, np.float32)
    for window, dil in BRANCHES:
        c += ((dist >= 0) & (dist <= window) & (dist % dil == 0)).astype(np.float32)
    return c


def _rms(x):
    return x * lax.rsqrt(jnp.mean(x * x, axis=-1, keepdims=True) + RMS_EPS)


def _norm_cast_kernel(x_ref, g_ref, o_ref):
    o_ref[...] = (_rms(x_ref[...]) * g_ref[...]).astype(o_ref.dtype)


def _norm_cast(x, g, tm):
    n, d = x.shape
    return pl.pallas_call(
        _norm_cast_kernel,
        grid=(n // tm,),
        in_specs=[pl.BlockSpec((tm, d), lambda i: (i, 0)),
                  pl.BlockSpec((1, d), lambda i: (0, 0))],
        out_specs=pl.BlockSpec((tm, d), lambda i: (i, 0)),
        out_shape=jax.ShapeDtypeStruct((n, d), BF16),
        compiler_params=_params(("parallel",)),
        name="norm_cast",
    )(x, g.reshape(1, d))


def _mm_kernel(x_ref, w_ref, o_ref):
    o_ref[...] = jnp.dot(x_ref[...], w_ref[...], preferred_element_type=F32)


def _mm_res_kernel(x_ref, w_ref, r_ref, o_ref):
    o_ref[...] = r_ref[...] + jnp.dot(x_ref[...], w_ref[...], preferred_element_type=F32)


def _matmul(x, w, residual, tm, tn):
    n, k = x.shape
    m = w.shape[1]
    in_specs = [pl.BlockSpec((tm, k), lambda j, i: (i, 0)),
                pl.BlockSpec((k, tn), lambda j, i: (0, j))]
    args = [x, w]
    body = _mm_kernel
    if residual is not None:
        in_specs.append(pl.BlockSpec((tm, tn), lambda j, i: (i, j)))
        args.append(residual)
        body = _mm_res_kernel
    return pl.pallas_call(
        body,
        grid=(m // tn, n // tm),
        in_specs=in_specs,
        out_specs=pl.BlockSpec((tm, tn), lambda j, i: (i, j)),
        out_shape=jax.ShapeDtypeStruct((n, m), F32),
        compiler_params=_params(("parallel", "parallel")),
        name="proj_res" if residual is not None else "proj",
    )(*args)


ATT_TQ = 256
ATT_CHUNKS = MAX_WINDOW // ATT_TQ + 1


def _prompt_attn_kernel(q_ref, k_ref, v_ref, c_ref, o_ref):
    qb = pl.program_id(2)
    q = (q_ref[...] * (HEAD_DIM ** -0.5)).astype(BF16)

    def chunk(c, carry):
        m, l, acc = carry
        start = pl.multiple_of((qb - c) * ATT_TQ, ATT_TQ)
        kc = k_ref[pl.ds(start, ATT_TQ), :].astype(BF16)
        vc = v_ref[pl.ds(start, ATT_TQ), :].astype(BF16)
        s = lax.dot_general(q, kc, (((1,), (1,)), ((), ())), preferred_element_type=F32)
        cw = c_ref[c]
        s = jnp.where(cw > 0, s, -jnp.inf)
        m_new = jnp.maximum(m, jnp.max(s, axis=-1, keepdims=True))
        p = cw * jnp.exp(s - m_new)
        alpha = jnp.exp(m - m_new)
        l = alpha * l + jnp.sum(p, axis=-1, keepdims=True)
        acc = alpha * acc + jnp.dot(p.astype(BF16), vc, preferred_element_type=F32)
        return m_new, l, acc

    init = (jnp.full((ATT_TQ, 1), -jnp.inf, F32), jnp.zeros((ATT_TQ, 1), F32),
            jnp.zeros((ATT_TQ, HEAD_DIM), F32))
    n_chunks = jnp.minimum(qb, ATT_CHUNKS - 1) + 1
    _, l, acc = lax.fori_loop(0, n_chunks, chunk, init)
    o_ref[...] = acc / l


def _prompt_attention(proj, batch, seq):
    qi = np.arange(ATT_TQ)[:, None]
    kj = np.arange(ATT_TQ)[None, :]
    table = np.stack([_branch_weight(qi - kj + ATT_TQ * c) for c in range(ATT_CHUNKS)])
    nqb = seq // ATT_TQ
    return pl.pallas_call(
        _prompt_attn_kernel,
        grid=(batch, N_HEADS, nqb),
        in_specs=[
            pl.BlockSpec((ATT_TQ, HEAD_DIM), lambda b, h, i: (b * nqb + i, h)),
            pl.BlockSpec((seq, HEAD_DIM), lambda b, h, i: (b, N_HEADS + h)),
            pl.BlockSpec((seq, HEAD_DIM), lambda b, h, i: (b, 2 * N_HEADS + h)),
            pl.BlockSpec((ATT_CHUNKS, ATT_TQ, ATT_TQ), lambda b, h, i: (0, 0, 0)),
        ],
        out_specs=pl.BlockSpec((ATT_TQ, HEAD_DIM), lambda b, h, i: (b * nqb + i, h)),
        out_shape=jax.ShapeDtypeStruct((batch * seq, D_ATT), F32),
        compiler_params=_params(("parallel", "parallel", "parallel")),
        name="prompt_attn",
    )(proj, proj, proj, jnp.asarray(table))


SA_HEADS = 8
SA_LANES = SA_HEADS * HEAD_DIM
SA_NEAR = 512
SA_NEW_PAD = 16
SA_KEYS = 1024


def _sample_attn_kernel(n_far, q_ref, kn_ref, vn_ref, *refs):
    t_new = q_ref.shape[1]
    n_res = t_new
    kfar = refs[0:n_res]
    knear = refs[n_res]
    vfar = refs[n_res + 1:2 * n_res + 1]
    vnear = refs[2 * n_res + 1]
    c_ref = refs[2 * n_res + 2]
    o_ref = refs[2 * n_res + 3]
    kb_ref, vb_ref = refs[2 * n_res + 4:]

    off = 0
    for r in range(n_res):
        kb_ref[off:off + n_far, :] = kfar[r][0].astype(BF16)
        vb_ref[off:off + n_far, :] = vfar[r][0].astype(BF16)
        off += n_far
    kb_ref[off:off + SA_NEAR, :] = knear[0].astype(BF16)
    vb_ref[off:off + SA_NEAR, :] = vnear[0].astype(BF16)
    off += SA_NEAR
    kb_ref[off:off + SA_NEW_PAD, :] = kn_ref[0].astype(BF16)
    vb_ref[off:off + SA_NEW_PAD, :] = vn_ref[0].astype(BF16)
    off += SA_NEW_PAD
    kb_ref[off:, :] = jnp.zeros((SA_KEYS - off, SA_LANES), BF16)
    vb_ref[off:, :] = jnp.zeros((SA_KEYS - off, SA_LANES), BF16)

    q = q_ref[0] * (HEAD_DIM ** -0.5)
    rows = t_new * SA_HEADS
    head_of_lane = lax.broadcasted_iota(jnp.int32, (rows, SA_LANES), 1) // HEAD_DIM
    head_of_row = lax.broadcasted_iota(jnp.int32, (rows, SA_LANES), 0) % SA_HEADS
    diag = head_of_lane == head_of_row
    q_rows = jnp.concatenate(
        [jnp.broadcast_to(q[t:t + 1, :], (SA_HEADS, SA_LANES)) for t in range(t_new)], axis=0)
    q_bd = jnp.where(diag, q_rows, 0.0).astype(BF16)

    s = lax.dot_general(q_bd, kb_ref[...], (((1,), (1,)), ((), ())), preferred_element_type=F32)
    cw = c_ref[...]
    s = jnp.where(cw > 0, s, -jnp.inf)
    m = jnp.max(s, axis=-1, keepdims=True)
    p = cw * jnp.exp(s - m)
    l = jnp.sum(p, axis=-1, keepdims=True)
    o = jnp.dot(p.astype(BF16), vb_ref[...], preferred_element_type=F32) / l
    o = jnp.where(diag, o, 0.0)
    o_ref[0] = jnp.concatenate(
        [jnp.sum(o[t * SA_HEADS:(t + 1) * SA_HEADS, :], axis=0, keepdims=True)
         for t in range(t_new)], axis=0)


def _sample_attention(proj_s, cache_k, cache_v):
    bsz, t_new, _ = proj_s.shape
    n_past = cache_k.shape[1]
    dil = BRANCHES[-1][1]
    assert n_past == MAX_WINDOW and n_past % dil == 0 and t_new <= dil
    assert all(w <= SA_NEAR for w, _ in BRANCHES[:-1])
    n_far = (n_past - SA_NEAR) // dil
    n_groups = n_past // dil
    assert n_far % 16 == 0 and t_new * n_far + SA_NEAR + SA_NEW_PAD <= SA_KEYS

    pos = np.full((SA_KEYS,), -1, np.int64)
    off = 0
    for r in range(t_new):
        pos[off:off + n_far] = dil * np.arange(n_far) + (n_past + r) % dil
        off += n_far
    pos[off:off + SA_NEAR] = n_past - SA_NEAR + np.arange(SA_NEAR)
    off += SA_NEAR
    pos[off:off + t_new] = n_past + np.arange(t_new)
    t_of_row = np.arange(t_new * SA_HEADS) // SA_HEADS
    dist = (n_past + t_of_row)[:, None] - pos[None, :]
    table = np.where(pos[None, :] >= 0, _branch_weight(dist), 0.0).astype(np.float32)

    pad = ((0, 0), (0, SA_NEW_PAD - t_new), (0, 0))
    k_new = jnp.pad(proj_s[:, :, D_ATT:2 * D_ATT], pad)
    v_new = jnp.pad(proj_s[:, :, 2 * D_ATT:3 * D_ATT], pad)
    groups_per_row = D_ATT // SA_LANES

    def far_spec(r):
        res = (n_past + r) % dil
        return pl.BlockSpec((1, n_far, SA_LANES),
                            lambda b, g, res=res: (b, 0, res * groups_per_row + g))

    near_spec = pl.BlockSpec((1, SA_NEAR, SA_LANES),
                             lambda b, g: (b, n_past // SA_NEAR - 1, g))
    new_spec = pl.BlockSpec((1, SA_NEW_PAD, SA_LANES), lambda b, g: (b, 0, g))
    ck_far = cache_k.reshape(bsz, n_groups, dil * D_ATT)
    cv_far = cache_v.reshape(bsz, n_groups, dil * D_ATT)
    in_specs = ([pl.BlockSpec((1, t_new, SA_LANES), lambda b, g: (b, 0, g)), new_spec, new_spec]
                + [far_spec(r) for r in range(t_new)] + [near_spec]
                + [far_spec(r) for r in range(t_new)] + [near_spec]
                + [pl.BlockSpec((t_new * SA_HEADS, SA_KEYS), lambda b, g: (0, 0))])
    args = ([proj_s, k_new, v_new] + [ck_far] * t_new + [cache_k] + [cv_far] * t_new + [cache_v]
            + [jnp.asarray(table)])
    return pl.pallas_call(
        functools.partial(_sample_attn_kernel, n_far),
        grid=(bsz, groups_per_row),
        in_specs=in_specs,
        out_specs=pl.BlockSpec((1, t_new, SA_LANES), lambda b, g: (b, 0, g)),
        out_shape=jax.ShapeDtypeStruct((bsz, t_new, D_ATT), F32),
        scratch_shapes=[pltpu.VMEM((SA_KEYS, SA_LANES), BF16),
                        pltpu.VMEM((SA_KEYS, SA_LANES), BF16)],
        compiler_params=_params(("parallel", "parallel")),
        name="sample_attn",
    )(*args)


CONV_TC = 128
CONV_HALO = 8


def _prompt_conv_kernel(b_ref, c_ref, x_ref, w_ref, o_ref, last_ref, u_ref):
    seq = b_ref.shape[0]
    u = c_ref[...] * x_ref[...]
    u_ref[0:CONV_HALO, :] = jnp.zeros((CONV_HALO, CONV_TC), F32)
    u_ref[CONV_HALO:, :] = u
    w = w_ref[...]
    acc = w[CONV_WIDTH - 1:CONV_WIDTH, :] * u
    for tap in range(CONV_WIDTH - 1):
        shift = CONV_WIDTH - 1 - tap
        acc = acc + w[tap:tap + 1, :] * u_ref[CONV_HALO - shift:CONV_HALO - shift + seq, :]
    o_ref[...] = b_ref[...] * acc
    last_ref[0] = u_ref[CONV_HALO + seq - (CONV_WIDTH - 1):, :]


def _prompt_conv(proj, conv_w, batch, seq):
    base = 3 * D_ATT // CONV_TC
    per = D_CONV // CONV_TC
    return pl.pallas_call(
        _prompt_conv_kernel,
        grid=(batch, per),
        in_specs=[pl.BlockSpec((seq, CONV_TC), lambda b, j: (b, base + j)),
                  pl.BlockSpec((seq, CONV_TC), lambda b, j: (b, base + per + j)),
                  pl.BlockSpec((seq, CONV_TC), lambda b, j: (b, base + 2 * per + j)),
                  pl.BlockSpec((CONV_WIDTH, CONV_TC), lambda b, j: (0, j))],
        out_specs=[pl.BlockSpec((seq, CONV_TC), lambda b, j: (b, j)),
                   pl.BlockSpec((1, CONV_WIDTH - 1, CONV_TC), lambda b, j: (b, 0, j))],
        out_shape=[jax.ShapeDtypeStruct((batch * seq, D_CONV), F32),
                   jax.ShapeDtypeStruct((batch, CONV_WIDTH - 1, D_CONV), F32)],
        scratch_shapes=[pltpu.VMEM((CONV_HALO + seq, CONV_TC), F32)],
        compiler_params=_params(("parallel", "parallel")),
        name="prompt_conv",
    )(proj, proj, proj, conv_w)


def _sample_conv_kernel(t_new, g_ref, st_ref, w_ref, o_ref, u_ref):
    w = w_ref[...]
    ext = [st_ref[:, i * D_CONV:(i + 1) * D_CONV] for i in range(CONV_WIDTH - 1)]
    for t in range(t_new):
        base = t * 3 * D_CONV
        u = g_ref[:, base + D_CONV:base + 2 * D_CONV] * g_ref[:, base + 2 * D_CONV:base + 3 * D_CONV]
        u_ref[:, t * D_CONV:(t + 1) * D_CONV] = u
        ext.append(u)
    for t in range(t_new):
        acc = w[0:1, :] * ext[t]
        for tap in range(1, CONV_WIDTH):
            acc = acc + w[tap:tap + 1, :] * ext[t + tap]
        o_ref[:, t * D_CONV:(t + 1) * D_CONV] = g_ref[:, t * 3 * D_CONV:t * 3 * D_CONV + D_CONV] * acc


def _sample_conv(gates, state, conv_w):
    bsz = gates.shape[0]
    t_new = gates.shape[1] // (3 * D_CONV)
    full = lambda a: pl.BlockSpec(a.shape, lambda i: (0,) * a.ndim)
    out = jax.ShapeDtypeStruct((bsz, t_new * D_CONV), F32)
    return pl.pallas_call(
        functools.partial(_sample_conv_kernel, t_new),
        grid=(1,),
        in_specs=[full(gates), full(state), full(conv_w)],
        out_specs=[pl.BlockSpec(out.shape, lambda i: (0, 0))] * 2,
        out_shape=[out, out],
        compiler_params=_params(("arbitrary",)),
        name="sample_conv",
    )(gates, state, conv_w)


def _mixer_norm_kernel(a_ref, c_ref, ga_ref, gc_ref, o_ref):
    o_ref[:, :D_ATT] = (_rms(a_ref[...]) * ga_ref[...]).astype(o_ref.dtype)
    o_ref[:, D_ATT:] = (_rms(c_ref[...]) * gc_ref[...]).astype(o_ref.dtype)


def _mixer_norm(attn, conv, g_attn, g_conv, tm):
    n = attn.shape[0]
    return pl.pallas_call(
        _mixer_norm_kernel,
        grid=(n // tm,),
        in_specs=[pl.BlockSpec((tm, D_ATT), lambda i: (i, 0)),
                  pl.BlockSpec((tm, D_CONV), lambda i: (i, 0)),
                  pl.BlockSpec((1, D_ATT), lambda i: (0, 0)),
                  pl.BlockSpec((1, D_CONV), lambda i: (0, 0))],
        out_specs=pl.BlockSpec((tm, D_MODEL), lambda i: (i, 0)),
        out_shape=jax.ShapeDtypeStruct((n, D_MODEL), BF16),
        compiler_params=_params(("parallel",)),
        name="mixer_norm",
    )(attn, conv, g_attn.reshape(1, D_ATT), g_conv.reshape(1, D_CONV))


def _router_kernel(h_ref, g_ref, wr_ref, br_ref, xn_ref, e_ref, gate_ref):
    xn = _rms(h_ref[...]) * g_ref[...]
    xn_ref[...] = xn.astype(xn_ref.dtype)
    logits = jnp.dot(xn, wr_ref[...], preferred_element_type=F32,
                     precision=lax.Precision.HIGHEST) + br_ref[...]
    lane = lax.broadcasted_iota(jnp.int32, logits.shape, 1)
    vals = []
    for k in range(TOP_K):
        m = jnp.max(logits, axis=-1, keepdims=True)
        idx = jnp.min(jnp.where(logits == m, lane, N_EXPERTS), axis=-1, keepdims=True)
        e_ref[:, k:k + 1] = idx
        vals.append(m)
        logits = jnp.where(lane == idx, -jnp.inf, logits)
    ex = [jnp.exp(v - vals[0]) for v in vals]
    den = ex[0]
    for k in range(1, TOP_K):
        den = den + ex[k]
    for k in range(TOP_K):
        gate_ref[:, k:k + 1] = ex[k] / den


def _router(h, g, w_router, b_router, tm):
    n = h.shape[0]
    return pl.pallas_call(
        _router_kernel,
        grid=(n // tm,),
        in_specs=[pl.BlockSpec((tm, D_MODEL), lambda i: (i, 0)),
                  pl.BlockSpec((1, D_MODEL), lambda i: (0, 0)),
                  pl.BlockSpec((D_MODEL, N_EXPERTS), lambda i: (0, 0)),
                  pl.BlockSpec((1, N_EXPERTS), lambda i: (0, 0))],
        out_specs=[pl.BlockSpec((tm, D_MODEL), lambda i: (i, 0)),
                   pl.BlockSpec((tm, TOP_K), lambda i: (i, 0)),
                   pl.BlockSpec((tm, TOP_K), lambda i: (i, 0))],
        out_shape=[jax.ShapeDtypeStruct((n, D_MODEL), BF16),
                   jax.ShapeDtypeStruct((n, TOP_K), jnp.int32),
                   jax.ShapeDtypeStruct((n, TOP_K), F32)],
        compiler_params=_params(("parallel",)),
        name="router",
    )(h, g.reshape(1, D_MODEL), w_router, b_router.reshape(1, N_EXPERTS))


MOE_TM = 512
MOE_TF = 128
MOE_TN = 1024


def _experts_kernel(te_ref, tv_ref, x_ref, wg_ref, wu_ref, bg_ref, bu_ref, wd_ref, bd_ref,
                    gate_ref, o_ref, wcat_ref):
    i = pl.program_id(0)
    j = pl.program_id(1)
    n_j = pl.num_programs(1)
    valid = tv_ref[i] > 0

    @pl.when(jnp.logical_and(jnp.logical_not(valid), j == 0))
    def _():
        o_ref[...] = jnp.zeros_like(o_ref)

    @pl.when(valid)
    def _():
        wcat_ref[:, :MOE_TF] = wg_ref[0].astype(BF16)
        wcat_ref[:, MOE_TF:] = wu_ref[0].astype(BF16)
        gu = jnp.dot(x_ref[...], wcat_ref[...], preferred_element_type=F32)
        glu = jnp.minimum(gu[:, :MOE_TF] + bg_ref[0], SWIGLU_LIMIT)
        lin = jnp.clip(gu[:, MOE_TF:] + bu_ref[0], -SWIGLU_LIMIT, SWIGLU_LIMIT)
        h = (glu * jax.nn.sigmoid(SWIGLU_ALPHA * glu) * (lin + 1.0)).astype(BF16)
        for n in range(D_MODEL // MOE_TN):
            cols = slice(n * MOE_TN, (n + 1) * MOE_TN)
            y = jnp.dot(h, wd_ref[0, :, cols].astype(BF16), preferred_element_type=F32)

            @pl.when(j == 0)
            def _():
                o_ref[:, cols] = y + bd_ref[0, :, cols]

            @pl.when(j > 0)
            def _():
                o_ref[:, cols] += y

        @pl.when(j == n_j - 1)
        def _():
            o_ref[...] = o_ref[...] * gate_ref[...]


def _experts(xs, slot_gate, tile_expert, tile_valid, w_gate_up, b_gate_up, w_down, b_down):
    n_slots = xs.shape[0]
    n_tiles = n_slots // MOE_TM
    n_j = D_FF // MOE_TF

    def jj(j, tv, i):
        return jnp.where(tv[i] > 0, j, n_j - 1)

    grid_spec = pltpu.PrefetchScalarGridSpec(
        num_scalar_prefetch=2,
        grid=(n_tiles, n_j),
        in_specs=[
            pl.BlockSpec((MOE_TM, D_MODEL), lambda i, j, te, tv: (i, 0)),
            pl.BlockSpec((1, D_MODEL, MOE_TF), lambda i, j, te, tv: (te[i], 0, jj(j, tv, i))),
            pl.BlockSpec((1, D_MODEL, MOE_TF), lambda i, j, te, tv: (te[i], 0, n_j + jj(j, tv, i))),
            pl.BlockSpec((1, 1, MOE_TF), lambda i, j, te, tv: (te[i], 0, jj(j, tv, i))),
            pl.BlockSpec((1, 1, MOE_TF), lambda i, j, te, tv: (te[i], 0, n_j + jj(j, tv, i))),
            pl.BlockSpec((1, MOE_TF, D_MODEL), lambda i, j, te, tv: (te[i], jj(j, tv, i), 0)),
            pl.BlockSpec((1, 1, D_MODEL), lambda i, j, te, tv: (te[i], 0, 0)),
            pl.BlockSpec((MOE_TM, 1), lambda i, j, te, tv: (i, 0)),
        ],
        out_specs=pl.BlockSpec((MOE_TM, D_MODEL), lambda i, j, te, tv: (i, 0)),
        scratch_shapes=[pltpu.VMEM((D_MODEL, 2 * MOE_TF), BF16)],
    )
    return pl.pallas_call(
        _experts_kernel,
        grid_spec=grid_spec,
        out_shape=jax.ShapeDtypeStruct((n_slots, D_MODEL), F32),
        compiler_params=_params(("arbitrary", "arbitrary")),
        name="experts",
    )(tile_expert, tile_valid, xs, w_gate_up, w_gate_up,
      b_gate_up.reshape(N_EXPERTS, 1, 2 * D_FF), b_gate_up.reshape(N_EXPERTS, 1, 2 * D_FF),
      w_down, b_down.reshape(N_EXPERTS, 1, D_MODEL), slot_gate)


def _dispatch(top_e, gate):
    n = top_e.shape[0]
    n_assign = n * TOP_K
    n_tiles = -(-n_assign // MOE_TM) + N_EXPERTS
    n_slots = n_tiles * MOE_TM
    flat_e = top_e.reshape(-1)
    onehot = (flat_e[:, None] == jnp.arange(N_EXPERTS, dtype=jnp.int32)[None, :]).astype(jnp.int32)
    csum = jnp.cumsum(onehot, axis=0)
    rank = jnp.sum(csum * onehot, axis=1) - 1
    counts = csum[-1]
    tiles = (counts + MOE_TM - 1) // MOE_TM
    tile_end = jnp.cumsum(tiles)
    pad_start = (tile_end - tiles) * MOE_TM
    dest = pad_start[flat_e] + rank
    slot_tok = jnp.zeros((n_slots,), jnp.int32).at[dest].set(
        jnp.arange(n_assign, dtype=jnp.int32) // TOP_K)
    slot_gate = jnp.zeros((n_slots,), F32).at[dest].set(gate.reshape(-1))
    tile_ids = jnp.arange(n_tiles, dtype=jnp.int32)
    tile_valid = (tile_ids < tile_end[-1]).astype(jnp.int32)
    last = jnp.maximum(tile_end[-1] - 1, 0)
    tile_expert = jnp.searchsorted(tile_end, jnp.minimum(tile_ids, last), side="right")
    tile_expert = jnp.minimum(tile_expert, N_EXPERTS - 1).astype(jnp.int32)
    return slot_tok, slot_gate.reshape(n_slots, 1), tile_expert, tile_valid, dest.reshape(n, TOP_K)


def _final_kernel(h_ref, y_ref, g_ref, o_ref):
    h = h_ref[...]
    for k in range(TOP_K):
        h = h + y_ref[:, k * D_MODEL:(k + 1) * D_MODEL]
    o_ref[...] = _rms(h) * g_ref[...]


def _final(h, y_rows, g, row0, n_rows, tm):
    blk0 = row0 // tm
    return pl.pallas_call(
        _final_kernel,
        grid=(n_rows // tm,),
        in_specs=[pl.BlockSpec((tm, D_MODEL), lambda i: (blk0 + i, 0)),
                  pl.BlockSpec((tm, TOP_K * D_MODEL), lambda i: (blk0 + i, 0)),
                  pl.BlockSpec((1, D_MODEL), lambda i: (0, 0))],
        out_specs=pl.BlockSpec((tm, D_MODEL), lambda i: (i, 0)),
        out_shape=jax.ShapeDtypeStruct((n_rows, D_MODEL), F32),
        compiler_params=_params(("parallel",)),
        name="final_norm",
    )(h, y_rows, g.reshape(1, D_MODEL))


def kernel(x_prompt, x_sample, cache_k, cache_v, state_conv, norm_mix, w_in, conv_w,
           norm_attn_out, norm_conv_out, w_out, norm_ffn, w_router, b_router,
           w_gate_up, b_gate_up, w_down, b_down, norm_final):
    batch, seq, _ = x_prompt.shape
    bsz, t_new, _ = x_sample.shape
    depth = w_in.shape[0]
    assert depth == 1 and t_new >= CONV_WIDTH - 1
    n_p, n_s = batch * seq, bsz * t_new
    n = n_p + n_s
    layer = 0

    x_all = jnp.concatenate([x_prompt.reshape(n_p, D_MODEL), x_sample.reshape(n_s, D_MODEL)], axis=0)
    xn = _norm_cast(x_all, norm_mix[layer], 512)
    proj = _matmul(xn, w_in[layer].astype(BF16), None, 512, 1024)
    proj_p = proj[:n_p].reshape(batch, seq, D_IN)
    proj_s = proj[n_p:].reshape(bsz, t_new, D_IN)

    attn_p = _prompt_attention(proj, batch, seq)
    attn_s = _sample_attention(proj_s, cache_k[layer].reshape(bsz, -1, D_ATT),
                               cache_v[layer].reshape(bsz, -1, D_ATT))
    conv_p, u_last_p = _prompt_conv(proj, conv_w[layer], batch, seq)
    conv_s, u_s = _sample_conv(proj_s[:, :, 3 * D_ATT:].reshape(bsz, t_new * 3 * D_CONV),
                               state_conv[layer].reshape(bsz, (CONV_WIDTH - 1) * D_CONV),
                               conv_w[layer])

    attn = jnp.concatenate([attn_p, attn_s.reshape(n_s, D_ATT)], axis=0)
    conv = jnp.concatenate([conv_p, conv_s.reshape(n_s, D_CONV)], axis=0)
    cat = _mixer_norm(attn, conv, norm_attn_out[layer], norm_conv_out[layer], 512)
    h1 = _matmul(cat, w_out[layer].astype(BF16), x_all, 512, 1024)

    xn2, top_e, gate = _router(h1, norm_ffn[layer], w_router[layer], b_router[layer], 512)
    slot_tok, slot_gate, tile_expert, tile_valid, dest = _dispatch(top_e, gate)
    xs = xn2[slot_tok]
    ys = _experts(xs, slot_gate, tile_expert, tile_valid, w_gate_up[layer], b_gate_up[layer],
                  w_down[layer], b_down[layer])
    y_rows = ys[dest.reshape(-1)].reshape(n, TOP_K * D_MODEL)
    y_prompt = _final(h1, y_rows, norm_final, 0, n_p, 128).reshape(batch, seq, D_MODEL)
    y_sample = _final(h1, y_rows, norm_final, n_p, n_s, 128).reshape(bsz, t_new, D_MODEL)

    n_keep = min(MAX_WINDOW, seq)
    k_prompt_win = proj_p[:, seq - n_keep:, D_ATT:2 * D_ATT].reshape(1, batch, n_keep, N_HEADS, HEAD_DIM)
    v_prompt_win = proj_p[:, seq - n_keep:, 2 * D_ATT:3 * D_ATT].reshape(1, batch, n_keep, N_HEADS, HEAD_DIM)
    conv_prompt = u_last_p[None]
    k_sample_new = proj_s[:, :, D_ATT:2 * D_ATT].reshape(1, bsz, t_new, N_HEADS, HEAD_DIM)
    v_sample_new = proj_s[:, :, 2 * D_ATT:3 * D_ATT].reshape(1, bsz, t_new, N_HEADS, HEAD_DIM)
    conv_sample = u_s.reshape(bsz, t_new, D_CONV)[:, t_new - (CONV_WIDTH - 1):][None]
    return (y_prompt, y_sample, k_prompt_win, v_prompt_win, conv_prompt,
            k_sample_new, v_sample_new, conv_sample)
```

```python
import functools

import numpy as np
import jax
import jax.numpy as jnp
from jax import lax
from jax.experimental import pallas as pl
from jax.experimental.pallas import tpu as pltpu

D_MODEL = 4096
HEAD_DIM = 128
N_HEADS = 24
D_ATT = N_HEADS * HEAD_DIM
D_CONV = D_MODEL - D_ATT
CONV_WIDTH = 3
BRANCHES = ((128, 1), (512, 4), (2048, 16))
MAX_WINDOW = 2048
D_IN = 3 * D_ATT + 3 * D_CONV
N_EXPERTS = 32
TOP_K = 4
D_FF = D_MODEL
SWIGLU_ALPHA = 1.702
SWIGLU_LIMIT = 7.0
RMS_EPS = 1e-5

F32 = jnp.float32
BF16 = jnp.bfloat16

VMEM_LIMIT_BYTES = 56 * 1024 * 1024


def _params(semantics):
    return pltpu.CompilerParams(dimension_semantics=semantics,
                                vmem_limit_bytes=VMEM_LIMIT_BYTES)


def _branch_weight(dist):
    dist = np.asarray(dist)
    c = np.zeros(dist.shape, np.float32)
    for window, dil in BRANCHES:
        c += ((dist >= 0) & (dist <= window) & (dist % dil == 0)).astype(np.float32)
    return c


def _rms(x):
    return x * lax.rsqrt(jnp.mean(x * x, axis=-1, keepdims=True) + RMS_EPS)


def _norm_cast_kernel(x_ref, g_ref, o_ref):
    o_ref[...] = (_rms(x_ref[...]) * g_ref[...]).astype(o_ref.dtype)


def _norm_cast(x, g, tm):
    n, d = x.shape
    return pl.pallas_call(
        _norm_cast_kernel,
        grid=(n // tm,),
        in_specs=[pl.BlockSpec((tm, d), lambda i: (i, 0)),
                  pl.BlockSpec((1, d), lambda i: (0, 0))],
        out_specs=pl.BlockSpec((tm, d), lambda i: (i, 0)),
        out_shape=jax.ShapeDtypeStruct((n, d), BF16),
        compiler_params=_params(("parallel",)),
        name="norm_cast",
    )(x, g.reshape(1, d))


def _mm_kernel(x_ref, w_ref, o_ref):
    o_ref[...] = jnp.dot(x_ref[...], w_ref[...], preferred_element_type=F32)


def _mm_res_kernel(x_ref, w_ref, r_ref, o_ref):
    o_ref[...] = r_ref[...] + jnp.dot(x_ref[...], w_ref[...], preferred_element_type=F32)


def _matmul(x, w, residual, tm, tn):
    n, k = x.shape
    m = w.shape[1]
    in_specs = [pl.BlockSpec((tm, k), lambda j, i: (i, 0)),
                pl.BlockSpec((k, tn), lambda j, i: (0, j))]
    args = [x, w]
    body = _mm_kernel
    if residual is not None:
        in_specs.append(pl.BlockSpec((tm, tn), lambda j, i: (i, j)))
        args.append(residual)
        body = _mm_res_kernel
    return pl.pallas_call(
        body,
        grid=(m // tn, n // tm),
        in_specs=in_specs,
        out_specs=pl.BlockSpec((tm, tn), lambda j, i: (i, j)),
        out_shape=jax.ShapeDtypeStruct((n, m), F32),
        compiler_params=_params(("parallel", "parallel")),
        name="proj_res" if residual is not None else "proj",
    )(*args)


ATT_TQ = 256
ATT_HEADS = 4
ATT_CHUNKS = MAX_WINDOW // ATT_TQ + 1


def _prompt_attn_kernel(q_ref, k_ref, v_ref, c_ref, o_ref):
    qb = pl.program_id(2)
    lanes = [slice(h * HEAD_DIM, (h + 1) * HEAD_DIM) for h in range(ATT_HEADS)]
    qs = [(q_ref[:, ln] * (HEAD_DIM ** -0.5)).astype(BF16) for ln in lanes]

    def chunk(c, carry):
        start = pl.multiple_of((qb - c) * ATT_TQ, ATT_TQ)
        cw = c_ref[c]
        out = []
        for q, ln, (m, l, acc) in zip(qs, lanes, carry):
            kc = k_ref[pl.ds(start, ATT_TQ), ln].astype(BF16)
            vc = v_ref[pl.ds(start, ATT_TQ), ln].astype(BF16)
            s = lax.dot_general(q, kc, (((1,), (1,)), ((), ())), preferred_element_type=F32)
            s = jnp.where(cw > 0, s, -jnp.inf)
            m_new = jnp.maximum(m, jnp.max(s, axis=-1, keepdims=True))
            p = cw * jnp.exp(s - m_new)
            alpha = jnp.exp(m - m_new)
            l = alpha * l + jnp.sum(p, axis=-1, keepdims=True)
            acc = alpha * acc + jnp.dot(p.astype(BF16), vc, preferred_element_type=F32)
            out.append((m_new, l, acc))
        return tuple(out)

    init = tuple((jnp.full((ATT_TQ, 1), -jnp.inf, F32), jnp.zeros((ATT_TQ, 1), F32),
                  jnp.zeros((ATT_TQ, HEAD_DIM), F32)) for _ in range(ATT_HEADS))
    n_chunks = jnp.minimum(qb, ATT_CHUNKS - 1) + 1
    final = lax.fori_loop(0, n_chunks, chunk, init)
    for ln, (_, l, acc) in zip(lanes, final):
        o_ref[:, ln] = acc / l


def _prompt_attention(proj, batch, seq):
    qi = np.arange(ATT_TQ)[:, None]
    kj = np.arange(ATT_TQ)[None, :]
    table = np.stack([_branch_weight(qi - kj + ATT_TQ * c) for c in range(ATT_CHUNKS)])
    nqb = seq // ATT_TQ
    groups = N_HEADS // ATT_HEADS
    width = ATT_HEADS * HEAD_DIM
    return pl.pallas_call(
        _prompt_attn_kernel,
        grid=(batch, groups, nqb),
        in_specs=[
            pl.BlockSpec((ATT_TQ, width), lambda b, h, i: (b * nqb + i, h)),
            pl.BlockSpec((seq, width), lambda b, h, i: (b, groups + h)),
            pl.BlockSpec((seq, width), lambda b, h, i: (b, 2 * groups + h)),
            pl.BlockSpec((ATT_CHUNKS, ATT_TQ, ATT_TQ), lambda b, h, i: (0, 0, 0)),
        ],
        out_specs=pl.BlockSpec((ATT_TQ, width), lambda b, h, i: (b * nqb + i, h)),
        out_shape=jax.ShapeDtypeStruct((batch * seq, D_ATT), F32),
        compiler_params=_params(("parallel", "parallel", "parallel")),
        name="prompt_attn",
    )(proj, proj, proj, jnp.asarray(table))


SA_HEADS = 8
SA_NEAR = 512
LANE = 128


def _sample_attn_kernel(q_ref, kn_ref, vn_ref, *refs):
    t_new = q_ref.shape[0]
    kfar = refs[0:t_new]
    knear = refs[t_new]
    vfar = refs[t_new + 1:2 * t_new + 1]
    vnear = refs[2 * t_new + 1]
    c_ref, o_ref, kb_ref, vb_ref = refs[2 * t_new + 2:]
    n_keys = kb_ref.shape[0]

    def put(dst, off, src):
        rows = src.shape[0] * SA_HEADS
        dst[off:off + rows, :] = src[...].reshape(rows, HEAD_DIM).astype(BF16)
        return off + rows

    off = 0
    for k_src, v_src in list(zip(kfar, vfar)) + [(knear, vnear), (kn_ref, vn_ref)]:
        put(kb_ref, off, k_src)
        off = put(vb_ref, off, v_src)
    kb_ref[off:, :] = jnp.zeros((n_keys - off, HEAD_DIM), BF16)
    vb_ref[off:, :] = jnp.zeros((n_keys - off, HEAD_DIM), BF16)

    rows = t_new * SA_HEADS
    q = (q_ref[...].reshape(rows, HEAD_DIM) * (HEAD_DIM ** -0.5)).astype(BF16)
    s = lax.dot_general(q, kb_ref[...], (((1,), (1,)), ((), ())), preferred_element_type=F32)
    cw = c_ref[...]
    s = jnp.where(cw > 0, s, -jnp.inf)
    m = jnp.max(s, axis=-1, keepdims=True)
    p = cw * jnp.exp(s - m)
    l = jnp.sum(p, axis=-1, keepdims=True)
    o = jnp.dot(p.astype(BF16), vb_ref[...], preferred_element_type=F32) / l
    o_ref[...] = o.reshape(t_new, SA_HEADS, HEAD_DIM)


def _sample_attention(proj_s, cache_k, cache_v):
    bsz, t_new = proj_s.shape[:2]
    n_past = cache_k.shape[1]
    dil = BRANCHES[-1][1]
    assert n_past == MAX_WINDOW and n_past % dil == 0 and t_new <= dil
    assert all(w <= SA_NEAR for w, _ in BRANCHES[:-1])
    n_far = (n_past - SA_NEAR) // dil
    n_groups = n_past // dil
    n_keys = -(-((t_new * n_far + SA_NEAR + t_new) * SA_HEADS) // LANE) * LANE

    pos_list = [dil * np.arange(n_far) + (n_past + r) % dil for r in range(t_new)]
    pos_list += [n_past - SA_NEAR + np.arange(SA_NEAR), n_past + np.arange(t_new)]
    pos_used = np.repeat(np.concatenate(pos_list), SA_HEADS)
    pos = np.full((n_keys,), -1, np.int64)
    pos[:pos_used.size] = pos_used
    head = np.arange(n_keys) % SA_HEADS
    row = np.arange(t_new * SA_HEADS)
    dist = (n_past + row // SA_HEADS)[:, None] - pos[None, :]
    keep = (pos[None, :] >= 0) & (head[None, :] == (row % SA_HEADS)[:, None])
    table = np.where(keep, _branch_weight(dist), 0.0).astype(np.float32)

    def new_spec(which):
        return pl.BlockSpec((None, t_new, SA_HEADS, HEAD_DIM),
                            lambda b, g, which=which: (b, 0, which * (N_HEADS // SA_HEADS) + g, 0))

    def far_spec(r):
        res = (n_past + r) % dil
        return pl.BlockSpec((None, n_far, None, SA_HEADS, HEAD_DIM),
                            lambda b, g, res=res: (b, 0, res, g, 0))

    near_spec = pl.BlockSpec((None, SA_NEAR, SA_HEADS, HEAD_DIM),
                             lambda b, g: (b, n_past // SA_NEAR - 1, g, 0))
    ck_far = cache_k.reshape(bsz, n_groups, dil, N_HEADS, HEAD_DIM)
    cv_far = cache_v.reshape(bsz, n_groups, dil, N_HEADS, HEAD_DIM)
    in_specs = ([new_spec(0), new_spec(1), new_spec(2)]
                + [far_spec(r) for r in range(t_new)] + [near_spec]
                + [far_spec(r) for r in range(t_new)] + [near_spec]
                + [pl.BlockSpec((t_new * SA_HEADS, n_keys), lambda b, g: (0, 0))])
    args = ([proj_s] * 3 + [ck_far] * t_new + [cache_k] + [cv_far] * t_new + [cache_v]
            + [jnp.asarray(table)])
    return pl.pallas_call(
        _sample_attn_kernel,
        grid=(bsz, N_HEADS // SA_HEADS),
        in_specs=in_specs,
        out_specs=pl.BlockSpec((None, t_new, SA_HEADS, HEAD_DIM), lambda b, g: (b, 0, g, 0)),
        out_shape=jax.ShapeDtypeStruct((bsz, t_new, N_HEADS, HEAD_DIM), F32),
        scratch_shapes=[pltpu.VMEM((n_keys, HEAD_DIM), BF16),
                        pltpu.VMEM((n_keys, HEAD_DIM), BF16)],
        compiler_params=_params(("parallel", "parallel")),
        name="sample_attn",
    )(*args)


CONV_TC = 128
CONV_HALO = 8


def _prompt_conv_kernel(b_ref, c_ref, x_ref, w_ref, o_ref, last_ref, u_ref):
    seq = b_ref.shape[0]
    u = c_ref[...] * x_ref[...]
    u_ref[0:CONV_HALO, :] = jnp.zeros((CONV_HALO, CONV_TC), F32)
    u_ref[CONV_HALO:, :] = u
    w = w_ref[...]
    acc = w[CONV_WIDTH - 1:CONV_WIDTH, :] * u
    for tap in range(CONV_WIDTH - 1):
        shift = CONV_WIDTH - 1 - tap
        acc = acc + w[tap:tap + 1, :] * u_ref[CONV_HALO - shift:CONV_HALO - shift + seq, :]
    o_ref[...] = b_ref[...] * acc
    last_ref[0] = u_ref[CONV_HALO + seq - (CONV_WIDTH - 1):, :]


def _prompt_conv(proj, conv_w, batch, seq):
    base = 3 * D_ATT // CONV_TC
    per = D_CONV // CONV_TC
    return pl.pallas_call(
        _prompt_conv_kernel,
        grid=(batch, per),
        in_specs=[pl.BlockSpec((seq, CONV_TC), lambda b, j: (b, base + j)),
                  pl.BlockSpec((seq, CONV_TC), lambda b, j: (b, base + per + j)),
                  pl.BlockSpec((seq, CONV_TC), lambda b, j: (b, base + 2 * per + j)),
                  pl.BlockSpec((CONV_WIDTH, CONV_TC), lambda b, j: (0, j))],
        out_specs=[pl.BlockSpec((seq, CONV_TC), lambda b, j: (b, j)),
                   pl.BlockSpec((1, CONV_WIDTH - 1, CONV_TC), lambda b, j: (b, 0, j))],
        out_shape=[jax.ShapeDtypeStruct((batch * seq, D_CONV), F32),
                   jax.ShapeDtypeStruct((batch, CONV_WIDTH - 1, D_CONV), F32)],
        scratch_shapes=[pltpu.VMEM((CONV_HALO + seq, CONV_TC), F32)],
        compiler_params=_params(("parallel", "parallel")),
        name="prompt_conv",
    )(proj, proj, proj, conv_w)


def _sample_conv_kernel(t_new, g_ref, st_ref, w_ref, o_ref, u_ref):
    w = w_ref[...]
    ext = [st_ref[:, i * D_CONV:(i + 1) * D_CONV] for i in range(CONV_WIDTH - 1)]
    for t in range(t_new):
        base = t * 3 * D_CONV
        u = g_ref[:, base + D_CONV:base + 2 * D_CONV] * g_ref[:, base + 2 * D_CONV:base + 3 * D_CONV]
        u_ref[:, t * D_CONV:(t + 1) * D_CONV] = u
        ext.append(u)
    for t in range(t_new):
        acc = w[0:1, :] * ext[t]
        for tap in range(1, CONV_WIDTH):
            acc = acc + w[tap:tap + 1, :] * ext[t + tap]
        o_ref[:, t * D_CONV:(t + 1) * D_CONV] = g_ref[:, t * 3 * D_CONV:t * 3 * D_CONV + D_CONV] * acc


def _sample_conv(gates, state, conv_w):
    bsz = gates.shape[0]
    t_new = gates.shape[1] // (3 * D_CONV)
    full = lambda a: pl.BlockSpec(a.shape, lambda i: (0,) * a.ndim)
    out = jax.ShapeDtypeStruct((bsz, t_new * D_CONV), F32)
    return pl.pallas_call(
        functools.partial(_sample_conv_kernel, t_new),
        grid=(1,),
        in_specs=[full(gates), full(state), full(conv_w)],
        out_specs=[pl.BlockSpec(out.shape, lambda i: (0, 0))] * 2,
        out_shape=[out, out],
        compiler_params=_params(("arbitrary",)),
        name="sample_conv",
    )(gates, state, conv_w)


def _mixer_norm_kernel(a_ref, c_ref, ga_ref, gc_ref, o_ref):
    o_ref[:, :D_ATT] = (_rms(a_ref[...]) * ga_ref[...]).astype(o_ref.dtype)
    o_ref[:, D_ATT:] = (_rms(c_ref[...]) * gc_ref[...]).astype(o_ref.dtype)


def _mixer_norm(attn, conv, g_attn, g_conv, tm):
    n = attn.shape[0]
    return pl.pallas_call(
        _mixer_norm_kernel,
        grid=(n // tm,),
        in_specs=[pl.BlockSpec((tm, D_ATT), lambda i: (i, 0)),
                  pl.BlockSpec((tm, D_CONV), lambda i: (i, 0)),
                  pl.BlockSpec((1, D_ATT), lambda i: (0, 0)),
                  pl.BlockSpec((1, D_CONV), lambda i: (0, 0))],
        out_specs=pl.BlockSpec((tm, D_MODEL), lambda i: (i, 0)),
        out_shape=jax.ShapeDtypeStruct((n, D_MODEL), BF16),
        compiler_params=_params(("parallel",)),
        name="mixer_norm",
    )(attn, conv, g_attn.reshape(1, D_ATT), g_conv.reshape(1, D_CONV))


def _router_kernel(h_ref, g_ref, wr_ref, br_ref, xn_ref, e_ref, gate_ref):
    xn = _rms(h_ref[...]) * g_ref[...]
    xn_ref[...] = xn.astype(xn_ref.dtype)
    logits = jnp.dot(xn, wr_ref[...], preferred_element_type=F32,
                     precision=lax.Precision.HIGHEST) + br_ref[...]
    lane = lax.broadcasted_iota(jnp.int32, logits.shape, 1)
    vals = []
    for k in range(TOP_K):
        m = jnp.max(logits, axis=-1, keepdims=True)
        idx = jnp.min(jnp.where(logits == m, lane, N_EXPERTS), axis=-1, keepdims=True)
        e_ref[:, k:k + 1] = idx
        vals.append(m)
        logits = jnp.where(lane == idx, -jnp.inf, logits)
    ex = [jnp.exp(v - vals[0]) for v in vals]
    den = ex[0]
    for k in range(1, TOP_K):
        den = den + ex[k]
    for k in range(TOP_K):
        gate_ref[:, k:k + 1] = ex[k] / den


def _router(h, g, w_router, b_router, tm):
    n = h.shape[0]
    return pl.pallas_call(
        _router_kernel,
        grid=(n // tm,),
        in_specs=[pl.BlockSpec((tm, D_MODEL), lambda i: (i, 0)),
                  pl.BlockSpec((1, D_MODEL), lambda i: (0, 0)),
                  pl.BlockSpec((D_MODEL, N_EXPERTS), lambda i: (0, 0)),
                  pl.BlockSpec((1, N_EXPERTS), lambda i: (0, 0))],
        out_specs=[pl.BlockSpec((tm, D_MODEL), lambda i: (i, 0)),
                   pl.BlockSpec((tm, TOP_K), lambda i: (i, 0)),
                   pl.BlockSpec((tm, TOP_K), lambda i: (i, 0))],
        out_shape=[jax.ShapeDtypeStruct((n, D_MODEL), BF16),
                   jax.ShapeDtypeStruct((n, TOP_K), jnp.int32),
                   jax.ShapeDtypeStruct((n, TOP_K), F32)],
        compiler_params=_params(("parallel",)),
        name="router",
    )(h, g.reshape(1, D_MODEL), w_router, b_router.reshape(1, N_EXPERTS))


MOE_TM = 576
MOE_TF = 256
MOE_TN = 256
MOE_NF = D_FF // MOE_TF
MOE_NN = D_MODEL // MOE_TN


def _experts_kernel(te_ref, tv_ref, x_ref, wg_ref, wu_ref, bg_ref, bu_ref, wd_ref, bd_ref,
                    gate_ref, o_ref, h_ref):
    i = pl.program_id(0)
    s = pl.program_id(1)
    valid = tv_ref[i] > 0

    @pl.when(jnp.logical_and(valid, s < MOE_NF))
    def _():
        x = x_ref[...]
        g = jnp.dot(x, wg_ref[...].astype(BF16), preferred_element_type=F32) + bg_ref[...]
        u = jnp.dot(x, wu_ref[...].astype(BF16), preferred_element_type=F32) + bu_ref[...]
        glu = jnp.minimum(g, SWIGLU_LIMIT)
        lin = jnp.clip(u, -SWIGLU_LIMIT, SWIGLU_LIMIT)
        h = (glu * jax.nn.sigmoid(SWIGLU_ALPHA * glu) * (lin + 1.0)).astype(BF16)
        for j in range(MOE_NF):
            @pl.when(s == j)
            def _():
                h_ref[:, j * MOE_TF:(j + 1) * MOE_TF] = h

    @pl.when(jnp.logical_and(valid, s >= MOE_NF))
    def _():
        y = jnp.dot(h_ref[...], wd_ref[...].astype(BF16), preferred_element_type=F32)
        o_ref[...] = (y + bd_ref[...]) * gate_ref[...]

    @pl.when(jnp.logical_and(jnp.logical_not(valid), s >= MOE_NF))
    def _():
        o_ref[...] = jnp.zeros_like(o_ref)


def _experts(xs, slot_gate, tile_expert, tile_valid, w_gate_up, b_gate_up, w_down, b_down):
    n_slots = xs.shape[0]
    n_tiles = n_slots // MOE_TM

    def jf(s, tv, i):
        return jnp.where(tv[i] > 0, jnp.minimum(s, MOE_NF - 1), MOE_NF - 1)

    def jn(s):
        return jnp.clip(s - MOE_NF, 0, MOE_NN - 1)

    def jn_w(s, tv, i):
        return jnp.where(tv[i] > 0, jn(s), MOE_NN - 1)

    grid_spec = pltpu.PrefetchScalarGridSpec(
        num_scalar_prefetch=2,
        grid=(n_tiles, MOE_NF + MOE_NN),
        in_specs=[
            pl.BlockSpec((MOE_TM, D_MODEL), lambda i, s, te, tv: (i, 0)),
            pl.BlockSpec((None, D_MODEL, MOE_TF), lambda i, s, te, tv: (te[i], 0, jf(s, tv, i))),
            pl.BlockSpec((None, D_MODEL, MOE_TF),
                         lambda i, s, te, tv: (te[i], 0, MOE_NF + jf(s, tv, i))),
            pl.BlockSpec((None, 1, MOE_TF), lambda i, s, te, tv: (te[i], 0, jf(s, tv, i))),
            pl.BlockSpec((None, 1, MOE_TF), lambda i, s, te, tv: (te[i], 0, MOE_NF + jf(s, tv, i))),
            pl.BlockSpec((None, D_FF, MOE_TN), lambda i, s, te, tv: (te[i], 0, jn_w(s, tv, i))),
            pl.BlockSpec((None, 1, MOE_TN), lambda i, s, te, tv: (te[i], 0, jn_w(s, tv, i))),
            pl.BlockSpec((MOE_TM, 1), lambda i, s, te, tv: (i, 0)),
        ],
        out_specs=pl.BlockSpec((MOE_TM, MOE_TN), lambda i, s, te, tv: (i, jn(s))),
        scratch_shapes=[pltpu.VMEM((MOE_TM, D_FF), BF16)],
    )
    return pl.pallas_call(
        _experts_kernel,
        grid_spec=grid_spec,
        out_shape=jax.ShapeDtypeStruct((n_slots, D_MODEL), F32),
        compiler_params=_params(("arbitrary", "arbitrary")),
        name="experts",
    )(tile_expert, tile_valid, xs, w_gate_up, w_gate_up,
      b_gate_up.reshape(N_EXPERTS, 1, 2 * D_FF), b_gate_up.reshape(N_EXPERTS, 1, 2 * D_FF),
      w_down, b_down.reshape(N_EXPERTS, 1, D_MODEL), slot_gate)


def _dispatch(top_e, gate):
    n = top_e.shape[0]
    n_assign = n * TOP_K
    n_tiles = -(-n_assign // MOE_TM) + N_EXPERTS
    n_slots = n_tiles * MOE_TM
    flat_e = top_e.reshape(-1)
    onehot = (flat_e[:, None] == jnp.arange(N_EXPERTS, dtype=jnp.int32)[None, :]).astype(jnp.int32)
    csum = jnp.cumsum(onehot, axis=0)
    rank = jnp.sum(csum * onehot, axis=1) - 1
    counts = csum[-1]
    tiles = (counts + MOE_TM - 1) // MOE_TM
    tile_end = jnp.cumsum(tiles)
    pad_start = (tile_end - tiles) * MOE_TM
    dest = pad_start[flat_e] + rank
    slot_tok = jnp.zeros((n_slots,), jnp.int32).at[dest].set(
        jnp.arange(n_assign, dtype=jnp.int32) // TOP_K)
    slot_gate = jnp.zeros((n_slots,), F32).at[dest].set(gate.reshape(-1))
    tile_ids = jnp.arange(n_tiles, dtype=jnp.int32)
    tile_valid = (tile_ids < tile_end[-1]).astype(jnp.int32)
    last = jnp.maximum(tile_end[-1] - 1, 0)
    tile_expert = jnp.sum(
        (tile_end[None, :] <= jnp.minimum(tile_ids, last)[:, None]).astype(jnp.int32), axis=1)
    tile_expert = jnp.minimum(tile_expert, N_EXPERTS - 1).astype(jnp.int32)
    return slot_tok, slot_gate.reshape(n_slots, 1), tile_expert, tile_valid, dest.reshape(n, TOP_K)


def _final_kernel(h_ref, *refs):
    y_refs, g_ref, o_ref = refs[:TOP_K], refs[TOP_K], refs[TOP_K + 1]
    h = h_ref[...]
    for y_ref in y_refs:
        h = h + y_ref[...]
    o_ref[...] = _rms(h) * g_ref[...]


def _final(h, y_rows, g, row0, n_rows, tm):
    n = h.shape[0]
    blk0 = row0 // tm
    y_specs = [pl.BlockSpec((tm, D_MODEL), lambda i, k=k: (k * (n // tm) + blk0 + i, 0))
               for k in range(TOP_K)]
    return pl.pallas_call(
        _final_kernel,
        grid=(n_rows // tm,),
        in_specs=[pl.BlockSpec((tm, D_MODEL), lambda i: (blk0 + i, 0))] + y_specs
                 + [pl.BlockSpec((1, D_MODEL), lambda i: (0, 0))],
        out_specs=pl.BlockSpec((tm, D_MODEL), lambda i: (i, 0)),
        out_shape=jax.ShapeDtypeStruct((n_rows, D_MODEL), F32),
        compiler_params=_params(("parallel",)),
        name="final_norm",
    )(h, *([y_rows] * TOP_K), g.reshape(1, D_MODEL))


def kernel(x_prompt, x_sample, cache_k, cache_v, state_conv, norm_mix, w_in, conv_w,
           norm_attn_out, norm_conv_out, w_out, norm_ffn, w_router, b_router,
           w_gate_up, b_gate_up, w_down, b_down, norm_final):
    batch, seq, _ = x_prompt.shape
    bsz, t_new, _ = x_sample.shape
    depth = w_in.shape[0]
    assert depth == 1 and t_new >= CONV_WIDTH - 1
    n_p, n_s = batch * seq, bsz * t_new
    n = n_p + n_s
    layer = 0

    x_all = jnp.concatenate([x_prompt.reshape(n_p, D_MODEL), x_sample.reshape(n_s, D_MODEL)], axis=0)
    xn = _norm_cast(x_all, norm_mix[layer], 512)
    proj = _matmul(xn, w_in[layer].astype(BF16), None, 512, 1024)
    proj_s = lax.slice(proj, (n_p, 0), (n, D_IN))
    proj_s4 = proj_s.reshape(bsz, t_new, D_IN // HEAD_DIM, HEAD_DIM)

    attn_p = _prompt_attention(proj, batch, seq)
    attn_s = _sample_attention(proj_s4, cache_k[layer], cache_v[layer])
    conv_p, u_last_p = _prompt_conv(proj, conv_w[layer], batch, seq)
    conv_s, u_s = _sample_conv(proj_s[:, 3 * D_ATT:].reshape(bsz, t_new * 3 * D_CONV),
                               state_conv[layer].reshape(bsz, (CONV_WIDTH - 1) * D_CONV),
                               conv_w[layer])

    attn = jnp.concatenate([attn_p, attn_s.reshape(n_s, D_ATT)], axis=0)
    conv = jnp.concatenate([conv_p, conv_s.reshape(n_s, D_CONV)], axis=0)
    cat = _mixer_norm(attn, conv, norm_attn_out[layer], norm_conv_out[layer], 512)
    h1 = _matmul(cat, w_out[layer].astype(BF16), x_all, 512, 1024)

    xn2, top_e, gate = _router(h1, norm_ffn[layer], w_router[layer], b_router[layer], 512)
    slot_tok, slot_gate, tile_expert, tile_valid, dest = _dispatch(top_e, gate)
    xs = xn2[slot_tok]
    ys = _experts(xs, slot_gate, tile_expert, tile_valid, w_gate_up[layer], b_gate_up[layer],
                  w_down[layer], b_down[layer])
    y_rows = ys[dest.T.reshape(-1)]
    y_prompt = _final(h1, y_rows, norm_final, 0, n_p, 128).reshape(batch, seq, D_MODEL)
    y_sample = _final(h1, y_rows, norm_final, n_p, n_s, 128).reshape(bsz, t_new, D_MODEL)

    n_keep = min(MAX_WINDOW, seq)

    def prompt_window(col0):
        parts = [lax.slice(proj, ((b + 1) * seq - n_keep, col0), ((b + 1) * seq, col0 + D_ATT))
                 for b in range(batch)]
        return jnp.stack(parts).reshape(1, batch, n_keep, N_HEADS, HEAD_DIM)

    k_prompt_win = prompt_window(D_ATT)
    v_prompt_win = prompt_window(2 * D_ATT)
    conv_prompt = u_last_p[None]
    k_sample_new = proj_s4[:, :, N_HEADS:2 * N_HEADS][None]
    v_sample_new = proj_s4[:, :, 2 * N_HEADS:3 * N_HEADS][None]
    conv_sample = u_s.reshape(bsz, t_new, D_CONV)[:, t_new - (CONV_WIDTH - 1):][None]
    return (y_prompt, y_sample, k_prompt_win, v_prompt_win, conv_prompt,
            k_sample_new, v_sample_new, conv_sample)
```

```python
import functools

import numpy as np
import jax
import jax.numpy as jnp
from jax import lax
from jax.experimental import pallas as pl
from jax.experimental.pallas import tpu as pltpu

D_MODEL = 4096
HEAD_DIM = 128
N_HEADS = 24
D_ATT = N_HEADS * HEAD_DIM
D_CONV = D_MODEL - D_ATT
CONV_WIDTH = 3
BRANCHES = ((128, 1), (512, 4), (2048, 16))
MAX_WINDOW = 2048
D_IN = 3 * D_ATT + 3 * D_CONV
N_EXPERTS = 32
TOP_K = 4
D_FF = D_MODEL
SWIGLU_ALPHA = 1.702
SWIGLU_LIMIT = 7.0
RMS_EPS = 1e-5

F32 = jnp.float32
BF16 = jnp.bfloat16

VMEM_LIMIT_BYTES = 56 * 1024 * 1024


def _params(semantics):
    return pltpu.CompilerParams(dimension_semantics=semantics,
                                vmem_limit_bytes=VMEM_LIMIT_BYTES)


def _branch_weight(dist):
    dist = np.asarray(dist)
    c = np.zeros(dist.shape, np.float32)
    for window, dil in BRANCHES:
        c += ((dist >= 0) & (dist <= window) & (dist % dil == 0)).astype(np.float32)
    return c


def _rms(x):
    return x * lax.rsqrt(jnp.mean(x * x, axis=-1, keepdims=True) + RMS_EPS)


def _norm_cast_kernel(x_ref, g_ref, o_ref):
    o_ref[...] = (_rms(x_ref[...]) * g_ref[...]).astype(o_ref.dtype)


def _norm_cast(x, g, tm):
    n, d = x.shape
    return pl.pallas_call(
        _norm_cast_kernel,
        grid=(n // tm,),
        in_specs=[pl.BlockSpec((tm, d), lambda i: (i, 0)),
                  pl.BlockSpec((1, d), lambda i: (0, 0))],
        out_specs=pl.BlockSpec((tm, d), lambda i: (i, 0)),
        out_shape=jax.ShapeDtypeStruct((n, d), BF16),
        compiler_params=_params(("parallel",)),
        name="norm_cast",
    )(x, g.reshape(1, d))


def _mm_kernel(x_ref, w_ref, o_ref):
    o_ref[...] = jnp.dot(x_ref[...], w_ref[...], preferred_element_type=F32)


def _mm_res_kernel(x_ref, w_ref, r_ref, o_ref):
    o_ref[...] = r_ref[...] + jnp.dot(x_ref[...], w_ref[...], preferred_element_type=F32)


def _matmul(x, w, residual, tm, tn):
    n, k = x.shape
    m = w.shape[1]
    in_specs = [pl.BlockSpec((tm, k), lambda j, i: (i, 0)),
                pl.BlockSpec((k, tn), lambda j, i: (0, j))]
    args = [x, w]
    body = _mm_kernel
    if residual is not None:
        in_specs.append(pl.BlockSpec((tm, tn), lambda j, i: (i, j)))
        args.append(residual)
        body = _mm_res_kernel
    return pl.pallas_call(
        body,
        grid=(m // tn, n // tm),
        in_specs=in_specs,
        out_specs=pl.BlockSpec((tm, tn), lambda j, i: (i, j)),
        out_shape=jax.ShapeDtypeStruct((n, m), F32),
        compiler_params=_params(("parallel", "parallel")),
        name="proj_res" if residual is not None else "proj",
    )(*args)


ATT_TQ = 256
ATT_HEADS = 4
ATT_CHUNKS = MAX_WINDOW // ATT_TQ + 1


def _prompt_attn_kernel(q_ref, k_ref, v_ref, c_ref, o_ref):
    qb = pl.program_id(2)
    lanes = [slice(h * HEAD_DIM, (h + 1) * HEAD_DIM) for h in range(ATT_HEADS)]
    qs = [(q_ref[:, ln] * (HEAD_DIM ** -0.5)).astype(BF16) for ln in lanes]

    def chunk(c, carry):
        start = pl.multiple_of((qb - c) * ATT_TQ, ATT_TQ)
        cw = c_ref[c]
        out = []
        for q, ln, (m, l, acc) in zip(qs, lanes, carry):
            kc = k_ref[pl.ds(start, ATT_TQ), ln].astype(BF16)
            vc = v_ref[pl.ds(start, ATT_TQ), ln].astype(BF16)
            s = lax.dot_general(q, kc, (((1,), (1,)), ((), ())), preferred_element_type=F32)
            s = jnp.where(cw > 0, s, -jnp.inf)
            m_new = jnp.maximum(m, jnp.max(s, axis=-1, keepdims=True))
            p = cw * jnp.exp(s - m_new)
            alpha = jnp.exp(m - m_new)
            l = alpha * l + jnp.sum(p, axis=-1, keepdims=True)
            acc = alpha * acc + jnp.dot(p.astype(BF16), vc, preferred_element_type=F32)
            out.append((m_new, l, acc))
        return tuple(out)

    init = tuple((jnp.full((ATT_TQ, 1), -jnp.inf, F32), jnp.zeros((ATT_TQ, 1), F32),
                  jnp.zeros((ATT_TQ, HEAD_DIM), F32)) for _ in range(ATT_HEADS))
    n_chunks = jnp.minimum(qb, ATT_CHUNKS - 1) + 1
    final = lax.fori_loop(0, n_chunks, chunk, init)
    for ln, (_, l, acc) in zip(lanes, final):
        o_ref[:, ln] = acc / l


def _prompt_attention(proj, batch, seq):
    qi = np.arange(ATT_TQ)[:, None]
    kj = np.arange(ATT_TQ)[None, :]
    table = np.stack([_branch_weight(qi - kj + ATT_TQ * c) for c in range(ATT_CHUNKS)])
    nqb = seq // ATT_TQ
    groups = N_HEADS // ATT_HEADS
    width = ATT_HEADS * HEAD_DIM
    return pl.pallas_call(
        _prompt_attn_kernel,
        grid=(batch, groups, nqb),
        in_specs=[
            pl.BlockSpec((ATT_TQ, width), lambda b, h, i: (b * nqb + i, h)),
            pl.BlockSpec((seq, width), lambda b, h, i: (b, groups + h)),
            pl.BlockSpec((seq, width), lambda b, h, i: (b, 2 * groups + h)),
            pl.BlockSpec((ATT_CHUNKS, ATT_TQ, ATT_TQ), lambda b, h, i: (0, 0, 0)),
        ],
        out_specs=pl.BlockSpec((ATT_TQ, width), lambda b, h, i: (b * nqb + i, h)),
        out_shape=jax.ShapeDtypeStruct((batch * seq, D_ATT), F32),
        compiler_params=_params(("parallel", "parallel", "parallel")),
        name="prompt_attn",
    )(proj, proj, proj, jnp.asarray(table))


SA_HEADS = 8
SA_NEAR = 512
LANE = 128


def _sample_attn_kernel(q_ref, kn_ref, vn_ref, *refs):
    t_new = q_ref.shape[0]
    kfar = refs[0:t_new]
    knear = refs[t_new]
    vfar = refs[t_new + 1:2 * t_new + 1]
    vnear = refs[2 * t_new + 1]
    c_ref, o_ref, kb_ref, vb_ref = refs[2 * t_new + 2:]
    n_keys = kb_ref.shape[0]

    def put(dst, off, src):
        rows = src.shape[0] * SA_HEADS
        dst[off:off + rows, :] = src[...].reshape(rows, HEAD_DIM).astype(BF16)
        return off + rows

    off = 0
    for k_src, v_src in list(zip(kfar, vfar)) + [(knear, vnear), (kn_ref, vn_ref)]:
        put(kb_ref, off, k_src)
        off = put(vb_ref, off, v_src)
    kb_ref[off:, :] = jnp.zeros((n_keys - off, HEAD_DIM), BF16)
    vb_ref[off:, :] = jnp.zeros((n_keys - off, HEAD_DIM), BF16)

    rows = t_new * SA_HEADS
    q = (q_ref[...].reshape(rows, HEAD_DIM) * (HEAD_DIM ** -0.5)).astype(BF16)
    s = lax.dot_general(q, kb_ref[...], (((1,), (1,)), ((), ())), preferred_element_type=F32)
    cw = c_ref[...]
    s = jnp.where(cw > 0, s, -jnp.inf)
    m = jnp.max(s, axis=-1, keepdims=True)
    p = cw * jnp.exp(s - m)
    l = jnp.sum(p, axis=-1, keepdims=True)
    o = jnp.dot(p.astype(BF16), vb_ref[...], preferred_element_type=F32) / l
    o_ref[...] = o.reshape(t_new, SA_HEADS, HEAD_DIM)


def _sample_attention(proj_s, cache_k, cache_v):
    bsz, t_new = proj_s.shape[:2]
    n_past = cache_k.shape[1]
    dil = BRANCHES[-1][1]
    assert n_past == MAX_WINDOW and n_past % dil == 0 and t_new <= dil
    assert all(w <= SA_NEAR for w, _ in BRANCHES[:-1])
    n_far = (n_past - SA_NEAR) // dil
    n_groups = n_past // dil
    n_keys = -(-((t_new * n_far + SA_NEAR + t_new) * SA_HEADS) // LANE) * LANE

    pos_list = [dil * np.arange(n_far) + (n_past + r) % dil for r in range(t_new)]
    pos_list += [n_past - SA_NEAR + np.arange(SA_NEAR), n_past + np.arange(t_new)]
    pos_used = np.repeat(np.concatenate(pos_list), SA_HEADS)
    pos = np.full((n_keys,), -1, np.int64)
    pos[:pos_used.size] = pos_used
    head = np.arange(n_keys) % SA_HEADS
    row = np.arange(t_new * SA_HEADS)
    dist = (n_past + row // SA_HEADS)[:, None] - pos[None, :]
    keep = (pos[None, :] >= 0) & (head[None, :] == (row % SA_HEADS)[:, None])
    table = np.where(keep, _branch_weight(dist), 0.0).astype(np.float32)

    def new_spec(which):
        return pl.BlockSpec((None, t_new, SA_HEADS, HEAD_DIM),
                            lambda b, g, which=which: (b, 0, which * (N_HEADS // SA_HEADS) + g, 0))

    def far_spec(r):
        res = (n_past + r) % dil
        return pl.BlockSpec((None, n_far, None, SA_HEADS, HEAD_DIM),
                            lambda b, g, res=res: (b, 0, res, g, 0))

    near_spec = pl.BlockSpec((None, SA_NEAR, SA_HEADS, HEAD_DIM),
                             lambda b, g: (b, n_past // SA_NEAR - 1, g, 0))
    ck_far = cache_k.reshape(bsz, n_groups, dil, N_HEADS, HEAD_DIM)
    cv_far = cache_v.reshape(bsz, n_groups, dil, N_HEADS, HEAD_DIM)
    in_specs = ([new_spec(0), new_spec(1), new_spec(2)]
                + [far_spec(r) for r in range(t_new)] + [near_spec]
                + [far_spec(r) for r in range(t_new)] + [near_spec]
                + [pl.BlockSpec((t_new * SA_HEADS, n_keys), lambda b, g: (0, 0))])
    args = ([proj_s] * 3 + [ck_far] * t_new + [cache_k] + [cv_far] * t_new + [cache_v]
            + [jnp.asarray(table)])
    return pl.pallas_call(
        _sample_attn_kernel,
        grid=(bsz, N_HEADS // SA_HEADS),
        in_specs=in_specs,
        out_specs=pl.BlockSpec((None, t_new, SA_HEADS, HEAD_DIM), lambda b, g: (b, 0, g, 0)),
        out_shape=jax.ShapeDtypeStruct((bsz, t_new, N_HEADS, HEAD_DIM), F32),
        scratch_shapes=[pltpu.VMEM((n_keys, HEAD_DIM), BF16),
                        pltpu.VMEM((n_keys, HEAD_DIM), BF16)],
        compiler_params=_params(("parallel", "parallel")),
        name="sample_attn",
    )(*args)


CONV_TC = 128
CONV_HALO = 8


def _prompt_conv_kernel(b_ref, c_ref, x_ref, w_ref, o_ref, last_ref, u_ref):
    seq = b_ref.shape[0]
    u = c_ref[...] * x_ref[...]
    u_ref[0:CONV_HALO, :] = jnp.zeros((CONV_HALO, CONV_TC), F32)
    u_ref[CONV_HALO:, :] = u
    w = w_ref[...]
    acc = w[CONV_WIDTH - 1:CONV_WIDTH, :] * u
    for tap in range(CONV_WIDTH - 1):
        shift = CONV_WIDTH - 1 - tap
        acc = acc + w[tap:tap + 1, :] * u_ref[CONV_HALO - shift:CONV_HALO - shift + seq, :]
    o_ref[...] = b_ref[...] * acc
    last_ref[0] = u_ref[CONV_HALO + seq - (CONV_WIDTH - 1):, :]


def _prompt_conv(proj, conv_w, batch, seq):
    base = 3 * D_ATT // CONV_TC
    per = D_CONV // CONV_TC
    return pl.pallas_call(
        _prompt_conv_kernel,
        grid=(batch, per),
        in_specs=[pl.BlockSpec((seq, CONV_TC), lambda b, j: (b, base + j)),
                  pl.BlockSpec((seq, CONV_TC), lambda b, j: (b, base + per + j)),
                  pl.BlockSpec((seq, CONV_TC), lambda b, j: (b, base + 2 * per + j)),
                  pl.BlockSpec((CONV_WIDTH, CONV_TC), lambda b, j: (0, j))],
        out_specs=[pl.BlockSpec((seq, CONV_TC), lambda b, j: (b, j)),
                   pl.BlockSpec((1, CONV_WIDTH - 1, CONV_TC), lambda b, j: (b, 0, j))],
        out_shape=[jax.ShapeDtypeStruct((batch * seq, D_CONV), F32),
                   jax.ShapeDtypeStruct((batch, CONV_WIDTH - 1, D_CONV), F32)],
        scratch_shapes=[pltpu.VMEM((CONV_HALO + seq, CONV_TC), F32)],
        compiler_params=_params(("parallel", "parallel")),
        name="prompt_conv",
    )(proj, proj, proj, conv_w)


def _sample_conv_kernel(t_new, g_ref, st_ref, w_ref, o_ref, u_ref):
    w = w_ref[...]
    ext = [st_ref[:, i * D_CONV:(i + 1) * D_CONV] for i in range(CONV_WIDTH - 1)]
    for t in range(t_new):
        base = t * 3 * D_CONV
        u = g_ref[:, base + D_CONV:base + 2 * D_CONV] * g_ref[:, base + 2 * D_CONV:base + 3 * D_CONV]
        u_ref[:, t * D_CONV:(t + 1) * D_CONV] = u
        ext.append(u)
    for t in range(t_new):
        acc = w[0:1, :] * ext[t]
        for tap in range(1, CONV_WIDTH):
            acc = acc + w[tap:tap + 1, :] * ext[t + tap]
        o_ref[:, t * D_CONV:(t + 1) * D_CONV] = g_ref[:, t * 3 * D_CONV:t * 3 * D_CONV + D_CONV] * acc


def _sample_conv(gates, state, conv_w):
    bsz = gates.shape[0]
    t_new = gates.shape[1] // (3 * D_CONV)
    full = lambda a: pl.BlockSpec(a.shape, lambda i: (0,) * a.ndim)
    out = jax.ShapeDtypeStruct((bsz, t_new * D_CONV), F32)
    return pl.pallas_call(
        functools.partial(_sample_conv_kernel, t_new),
        grid=(1,),
        in_specs=[full(gates), full(state), full(conv_w)],
        out_specs=[pl.BlockSpec(out.shape, lambda i: (0, 0))] * 2,
        out_shape=[out, out],
        compiler_params=_params(("arbitrary",)),
        name="sample_conv",
    )(gates, state, conv_w)


def _mixer_norm_kernel(a_ref, c_ref, ga_ref, gc_ref, o_ref):
    o_ref[:, :D_ATT] = (_rms(a_ref[...]) * ga_ref[...]).astype(o_ref.dtype)
    o_ref[:, D_ATT:] = (_rms(c_ref[...]) * gc_ref[...]).astype(o_ref.dtype)


def _mixer_norm(attn, conv, g_attn, g_conv, tm):
    n = attn.shape[0]
    return pl.pallas_call(
        _mixer_norm_kernel,
        grid=(n // tm,),
        in_specs=[pl.BlockSpec((tm, D_ATT), lambda i: (i, 0)),
                  pl.BlockSpec((tm, D_CONV), lambda i: (i, 0)),
                  pl.BlockSpec((1, D_ATT), lambda i: (0, 0)),
                  pl.BlockSpec((1, D_CONV), lambda i: (0, 0))],
        out_specs=pl.BlockSpec((tm, D_MODEL), lambda i: (i, 0)),
        out_shape=jax.ShapeDtypeStruct((n, D_MODEL), BF16),
        compiler_params=_params(("parallel",)),
        name="mixer_norm",
    )(attn, conv, g_attn.reshape(1, D_ATT), g_conv.reshape(1, D_CONV))


HALF = D_MODEL // 2


def _pack_bf16_pairs(x):
    bits = pltpu.bitcast(x.astype(BF16).astype(F32), jnp.uint32)
    return (bits[:, :HALF] >> 16) | bits[:, HALF:]


def _unpack_bf16_pairs(p):
    lo = pltpu.bitcast(p << 16, F32).astype(BF16)
    hi = pltpu.bitcast(p & jnp.uint32(0xFFFF0000), F32).astype(BF16)
    return lo, hi


def _router_kernel(h_ref, g_ref, wr_ref, br_ref, xn_ref, e_ref, gate_ref):
    xn = _rms(h_ref[...]) * g_ref[...]
    xn_ref[...] = _pack_bf16_pairs(xn)
    logits = jnp.dot(xn, wr_ref[...], preferred_element_type=F32,
                     precision=lax.Precision.HIGHEST) + br_ref[...]
    lane = lax.broadcasted_iota(jnp.int32, logits.shape, 1)
    vals = []
    for k in range(TOP_K):
        m = jnp.max(logits, axis=-1, keepdims=True)
        idx = jnp.min(jnp.where(logits == m, lane, N_EXPERTS), axis=-1, keepdims=True)
        e_ref[:, k:k + 1] = idx
        vals.append(m)
        logits = jnp.where(lane == idx, -jnp.inf, logits)
    ex = [jnp.exp(v - vals[0]) for v in vals]
    den = ex[0]
    for k in range(1, TOP_K):
        den = den + ex[k]
    for k in range(TOP_K):
        gate_ref[:, k:k + 1] = ex[k] / den


def _router(h, g, w_router, b_router, tm):
    n = h.shape[0]
    return pl.pallas_call(
        _router_kernel,
        grid=(n // tm,),
        in_specs=[pl.BlockSpec((tm, D_MODEL), lambda i: (i, 0)),
                  pl.BlockSpec((1, D_MODEL), lambda i: (0, 0)),
                  pl.BlockSpec((D_MODEL, N_EXPERTS), lambda i: (0, 0)),
                  pl.BlockSpec((1, N_EXPERTS), lambda i: (0, 0))],
        out_specs=[pl.BlockSpec((tm, HALF), lambda i: (i, 0)),
                   pl.BlockSpec((tm, TOP_K), lambda i: (i, 0)),
                   pl.BlockSpec((tm, TOP_K), lambda i: (i, 0))],
        out_shape=[jax.ShapeDtypeStruct((n, HALF), jnp.uint32),
                   jax.ShapeDtypeStruct((n, TOP_K), jnp.int32),
                   jax.ShapeDtypeStruct((n, TOP_K), F32)],
        compiler_params=_params(("parallel",)),
        name="router",
    )(h, g.reshape(1, D_MODEL), w_router, b_router.reshape(1, N_EXPERTS))


MOE_TM = 1152
MOE_TF = 128
MOE_TN = 256
MOE_KS = 2
MOE_KD = 4
MOE_NF = D_FF // MOE_TF
MOE_NN = D_MODEL // MOE_TN


def _experts_kernel(te_ref, tv_ref, xp_ref, *refs):
    ks, kd = MOE_KS, MOE_KD
    wg, wu = refs[0:ks], refs[ks:2 * ks]
    bg_ref, bu_ref = refs[2 * ks], refs[2 * ks + 1]
    wd = refs[2 * ks + 2:2 * ks + 2 + kd]
    bd_ref, gate_ref, o_ref, x_ref, h_ref, wgu_ref, wdn_ref = refs[2 * ks + 2 + kd:]
    i = pl.program_id(0)
    s = pl.program_id(1)
    valid = tv_ref[i] > 0

    @pl.when(jnp.logical_and(valid, s == 0))
    def _():
        lo, hi = _unpack_bf16_pairs(xp_ref[...])
        x_ref[:, :HALF] = lo
        x_ref[:, HALF:] = hi

    @pl.when(jnp.logical_and(valid, s < MOE_NF))
    def _():
        kr = D_MODEL // ks
        for r in range(ks):
            wgu_ref[r * kr:(r + 1) * kr, :MOE_TF] = wg[r][...].astype(BF16)
            wgu_ref[r * kr:(r + 1) * kr, MOE_TF:] = wu[r][...].astype(BF16)
        gu = jnp.dot(x_ref[...], wgu_ref[...], preferred_element_type=F32)
        glu = jnp.minimum(gu[:, :MOE_TF] + bg_ref[...], SWIGLU_LIMIT)
        lin = jnp.clip(gu[:, MOE_TF:] + bu_ref[...], -SWIGLU_LIMIT, SWIGLU_LIMIT)
        h = (glu * jax.nn.sigmoid(SWIGLU_ALPHA * glu) * (lin + 1.0)).astype(BF16)
        for j in range(MOE_NF):
            @pl.when(s == j)
            def _():
                h_ref[:, j * MOE_TF:(j + 1) * MOE_TF] = h

    @pl.when(jnp.logical_and(valid, s >= MOE_NF))
    def _():
        kr = D_FF // kd
        for r in range(kd):
            wdn_ref[r * kr:(r + 1) * kr, :] = wd[r][...].astype(BF16)
        y = jnp.dot(h_ref[...], wdn_ref[...], preferred_element_type=F32)
        o_ref[...] = (y + bd_ref[...]) * gate_ref[...]

    @pl.when(jnp.logical_and(jnp.logical_not(valid), s >= MOE_NF))
    def _():
        o_ref[...] = jnp.zeros_like(o_ref)


def _experts(xs, slot_gate, tile_expert, tile_valid, w_gate_up, b_gate_up, w_down, b_down):
    n_slots = xs.shape[0]
    n_tiles = n_slots // MOE_TM

    def jf(s, tv, i):
        return jnp.where(tv[i] > 0, jnp.minimum(s, MOE_NF - 1), MOE_NF - 1)

    def jn(s):
        return jnp.clip(s - MOE_NF, 0, MOE_NN - 1)

    def jn_w(s, tv, i):
        return jnp.where(tv[i] > 0, jn(s), MOE_NN - 1)

    def gate_up_spec(r, col0):
        return pl.BlockSpec((None, D_MODEL // MOE_KS, MOE_TF),
                            lambda i, s, te, tv: (te[i], r, col0 + jf(s, tv, i)))

    def down_spec(r):
        return pl.BlockSpec((None, D_FF // MOE_KD, MOE_TN),
                            lambda i, s, te, tv: (te[i], r, jn_w(s, tv, i)))

    grid_spec = pltpu.PrefetchScalarGridSpec(
        num_scalar_prefetch=2,
        grid=(n_tiles, MOE_NF + MOE_NN),
        in_specs=(
            [pl.BlockSpec((MOE_TM, HALF), lambda i, s, te, tv: (i, 0),
                          pipeline_mode=pl.Buffered(1))]
            + [gate_up_spec(r, 0) for r in range(MOE_KS)]
            + [gate_up_spec(r, MOE_NF) for r in range(MOE_KS)]
            + [pl.BlockSpec((None, 1, MOE_TF), lambda i, s, te, tv: (te[i], 0, jf(s, tv, i))),
               pl.BlockSpec((None, 1, MOE_TF),
                            lambda i, s, te, tv: (te[i], 0, MOE_NF + jf(s, tv, i)))]
            + [down_spec(r) for r in range(MOE_KD)]
            + [pl.BlockSpec((None, 1, MOE_TN), lambda i, s, te, tv: (te[i], 0, jn_w(s, tv, i))),
               pl.BlockSpec((MOE_TM, 1), lambda i, s, te, tv: (i, 0))]
        ),
        out_specs=pl.BlockSpec((MOE_TM, MOE_TN), lambda i, s, te, tv: (i, jn(s))),
        scratch_shapes=[pltpu.VMEM((MOE_TM, D_MODEL), BF16),
                        pltpu.VMEM((MOE_TM, D_FF), BF16),
                        pltpu.VMEM((D_MODEL, 2 * MOE_TF), BF16),
                        pltpu.VMEM((D_FF, MOE_TN), BF16)],
    )
    b_gu = b_gate_up.reshape(N_EXPERTS, 1, 2 * D_FF)
    return pl.pallas_call(
        _experts_kernel,
        grid_spec=grid_spec,
        out_shape=jax.ShapeDtypeStruct((n_slots, D_MODEL), F32),
        compiler_params=_params(("arbitrary", "arbitrary")),
        name="experts",
    )(tile_expert, tile_valid, xs, *([w_gate_up] * (2 * MOE_KS)), b_gu, b_gu,
      *([w_down] * MOE_KD), b_down.reshape(N_EXPERTS, 1, D_MODEL), slot_gate)


def _dispatch(top_e, gate):
    n = top_e.shape[0]
    n_assign = n * TOP_K
    n_tiles = -(-n_assign // MOE_TM) + N_EXPERTS
    n_slots = n_tiles * MOE_TM
    flat_e = top_e.reshape(-1)
    onehot = (flat_e[:, None] == jnp.arange(N_EXPERTS, dtype=jnp.int32)[None, :]).astype(jnp.int32)
    csum = jnp.cumsum(onehot, axis=0)
    rank = jnp.sum(csum * onehot, axis=1) - 1
    counts = csum[-1]
    tiles = (counts + MOE_TM - 1) // MOE_TM
    tile_end = jnp.cumsum(tiles)
    pad_start = (tile_end - tiles) * MOE_TM
    dest = pad_start[flat_e] + rank
    tok = jnp.arange(n_assign, dtype=jnp.int32) // TOP_K
    gate_bits = lax.bitcast_convert_type(gate.reshape(-1), jnp.int32)
    slots = jnp.zeros((n_slots, 2), jnp.int32).at[dest].set(jnp.stack([tok, gate_bits], axis=1))
    slot_tok = slots[:, 0]
    slot_gate = lax.bitcast_convert_type(slots[:, 1], F32)
    tile_ids = jnp.arange(n_tiles, dtype=jnp.int32)
    tile_valid = (tile_ids < tile_end[-1]).astype(jnp.int32)
    last = jnp.maximum(tile_end[-1] - 1, 0)
    tile_expert = jnp.sum(
        (tile_end[None, :] <= jnp.minimum(tile_ids, last)[:, None]).astype(jnp.int32), axis=1)
    tile_expert = jnp.minimum(tile_expert, N_EXPERTS - 1).astype(jnp.int32)
    return slot_tok, slot_gate.reshape(n_slots, 1), tile_expert, tile_valid, dest.reshape(n, TOP_K)


def _final_kernel(h_ref, *refs):
    y_refs, g_ref, o_ref = refs[:TOP_K], refs[TOP_K], refs[TOP_K + 1]
    h = h_ref[...]
    for y_ref in y_refs:
        h = h + y_ref[...]
    o_ref[...] = _rms(h) * g_ref[...]


def _final(h, y_rows, g, row0, n_rows, tm):
    n = h.shape[0]
    blk0 = row0 // tm
    y_specs = [pl.BlockSpec((tm, D_MODEL), lambda i, k=k: (k * (n // tm) + blk0 + i, 0))
               for k in range(TOP_K)]
    return pl.pallas_call(
        _final_kernel,
        grid=(n_rows // tm,),
        in_specs=[pl.BlockSpec((tm, D_MODEL), lambda i: (blk0 + i, 0))] + y_specs
                 + [pl.BlockSpec((1, D_MODEL), lambda i: (0, 0))],
        out_specs=pl.BlockSpec((tm, D_MODEL), lambda i: (i, 0)),
        out_shape=jax.ShapeDtypeStruct((n_rows, D_MODEL), F32),
        compiler_params=_params(("parallel",)),
        name="final_norm",
    )(h, *([y_rows] * TOP_K), g.reshape(1, D_MODEL))


def kernel(x_prompt, x_sample, cache_k, cache_v, state_conv, norm_mix, w_in, conv_w,
           norm_attn_out, norm_conv_out, w_out, norm_ffn, w_router, b_router,
           w_gate_up, b_gate_up, w_down, b_down, norm_final):
    batch, seq, _ = x_prompt.shape
    bsz, t_new, _ = x_sample.shape
    depth = w_in.shape[0]
    assert depth == 1 and t_new >= CONV_WIDTH - 1
    n_p, n_s = batch * seq, bsz * t_new
    n = n_p + n_s
    layer = 0

    x_all = jnp.concatenate([x_prompt.reshape(n_p, D_MODEL), x_sample.reshape(n_s, D_MODEL)], axis=0)
    xn = _norm_cast(x_all, norm_mix[layer], 512)
    proj = _matmul(xn, w_in[layer].astype(BF16), None, 512, 1024)
    proj_s = lax.slice(proj, (n_p, 0), (n, D_IN))
    proj_s4 = proj_s.reshape(bsz, t_new, D_IN // HEAD_DIM, HEAD_DIM)

    attn_p = _prompt_attention(proj, batch, seq)
    attn_s = _sample_attention(proj_s4, cache_k[layer], cache_v[layer])
    conv_p, u_last_p = _prompt_conv(proj, conv_w[layer], batch, seq)
    conv_s, u_s = _sample_conv(proj_s[:, 3 * D_ATT:].reshape(bsz, t_new * 3 * D_CONV),
                               state_conv[layer].reshape(bsz, (CONV_WIDTH - 1) * D_CONV),
                               conv_w[layer])

    attn = jnp.concatenate([attn_p, attn_s.reshape(n_s, D_ATT)], axis=0)
    conv = jnp.concatenate([conv_p, conv_s.reshape(n_s, D_CONV)], axis=0)
    cat = _mixer_norm(attn, conv, norm_attn_out[layer], norm_conv_out[layer], 512)
    h1 = _matmul(cat, w_out[layer].astype(BF16), x_all, 512, 1024)

    xn2, top_e, gate = _router(h1, norm_ffn[layer], w_router[layer], b_router[layer], 512)
    slot_tok, slot_gate, tile_expert, tile_valid, dest = _dispatch(top_e, gate)
    xs = xn2[slot_tok]
    ys = _experts(xs, slot_gate, tile_expert, tile_valid, w_gate_up[layer], b_gate_up[layer],
                  w_down[layer], b_down[layer])
    y_rows = ys[dest.T.reshape(-1)]
    y_prompt = _final(h1, y_rows, norm_final, 0, n_p, 128).reshape(batch, seq, D_MODEL)
    y_sample = _final(h1, y_rows, norm_final, n_p, n_s, 128).reshape(bsz, t_new, D_MODEL)

    n_keep = min(MAX_WINDOW, seq)

    def prompt_window(col0):
        parts = [lax.slice(proj, ((b + 1) * seq - n_keep, col0), ((b + 1) * seq, col0 + D_ATT))
                 for b in range(batch)]
        return jnp.stack(parts).reshape(1, batch, n_keep, N_HEADS, HEAD_DIM)

    k_prompt_win = prompt_window(D_ATT)
    v_prompt_win = prompt_window(2 * D_ATT)
    conv_prompt = u_last_p[None]
    k_sample_new = proj_s4[:, :, N_HEADS:2 * N_HEADS][None]
    v_sample_new = proj_s4[:, :, 2 * N_HEADS:3 * N_HEADS][None]
    conv_sample = u_s.reshape(bsz, t_new, D_CONV)[:, t_new - (CONV_WIDTH - 1):][None]
    return (y_prompt, y_sample, k_prompt_win, v_prompt_win, conv_prompt,
            k_sample_new, v_sample_new, conv_sample)
```

```python
import functools

import numpy as np
import jax
import jax.numpy as jnp
from jax import lax
from jax.experimental import pallas as pl
from jax.experimental.pallas import tpu as pltpu

D_MODEL = 4096
HEAD_DIM = 128
N_HEADS = 24
D_ATT = N_HEADS * HEAD_DIM
D_CONV = D_MODEL - D_ATT
CONV_WIDTH = 3
BRANCHES = ((128, 1), (512, 4), (2048, 16))
MAX_WINDOW = 2048
D_IN = 3 * D_ATT + 3 * D_CONV
N_EXPERTS = 32
TOP_K = 4
D_FF = D_MODEL
SWIGLU_ALPHA = 1.702
SWIGLU_LIMIT = 7.0
RMS_EPS = 1e-5

F32 = jnp.float32
BF16 = jnp.bfloat16

VMEM_LIMIT_BYTES = 56 * 1024 * 1024


def _params(semantics):
    return pltpu.CompilerParams(dimension_semantics=semantics,
                                vmem_limit_bytes=VMEM_LIMIT_BYTES)


def _branch_weight(dist):
    dist = np.asarray(dist)
    c = np.zeros(dist.shape, np.float32)
    for window, dil in BRANCHES:
        c += ((dist >= 0) & (dist <= window) & (dist % dil == 0)).astype(np.float32)
    return c


def _rms(x):
    return x * lax.rsqrt(jnp.mean(x * x, axis=-1, keepdims=True) + RMS_EPS)


def _norm_cast_kernel(x_ref, g_ref, o_ref):
    o_ref[...] = (_rms(x_ref[...]) * g_ref[...]).astype(o_ref.dtype)


def _norm_cast(x, g, tm):
    n, d = x.shape
    return pl.pallas_call(
        _norm_cast_kernel,
        grid=(n // tm,),
        in_specs=[pl.BlockSpec((tm, d), lambda i: (i, 0)),
                  pl.BlockSpec((1, d), lambda i: (0, 0))],
        out_specs=pl.BlockSpec((tm, d), lambda i: (i, 0)),
        out_shape=jax.ShapeDtypeStruct((n, d), BF16),
        compiler_params=_params(("parallel",)),
        name="norm_cast",
    )(x, g.reshape(1, d))


def _mm_kernel(x_ref, w_ref, o_ref):
    o_ref[...] = jnp.dot(x_ref[...], w_ref[...], preferred_element_type=F32)


def _mm_res_kernel(x_ref, w_ref, r_ref, o_ref):
    o_ref[...] = r_ref[...] + jnp.dot(x_ref[...], w_ref[...], preferred_element_type=F32)


def _matmul(x, w, residual, tm, tn):
    n, k = x.shape
    m = w.shape[1]
    in_specs = [pl.BlockSpec((tm, k), lambda j, i: (i, 0)),
                pl.BlockSpec((k, tn), lambda j, i: (0, j))]
    args = [x, w]
    body = _mm_kernel
    if residual is not None:
        in_specs.append(pl.BlockSpec((tm, tn), lambda j, i: (i, j)))
        args.append(residual)
        body = _mm_res_kernel
    return pl.pallas_call(
        body,
        grid=(m // tn, n // tm),
        in_specs=in_specs,
        out_specs=pl.BlockSpec((tm, tn), lambda j, i: (i, j)),
        out_shape=jax.ShapeDtypeStruct((n, m), F32),
        compiler_params=_params(("parallel", "parallel")),
        name="proj_res" if residual is not None else "proj",
    )(*args)


ATT_TQ = 256
ATT_HEADS = 4
ATT_CHUNKS = MAX_WINDOW // ATT_TQ + 1


def _prompt_attn_kernel(q_ref, k_ref, v_ref, c_ref, o_ref):
    qb = pl.program_id(2)
    lanes = [slice(h * HEAD_DIM, (h + 1) * HEAD_DIM) for h in range(ATT_HEADS)]
    qs = [(q_ref[:, ln] * (HEAD_DIM ** -0.5)).astype(BF16) for ln in lanes]

    def chunk(c, carry):
        start = pl.multiple_of((qb - c) * ATT_TQ, ATT_TQ)
        cw = c_ref[c]
        out = []
        for q, ln, (m, l, acc) in zip(qs, lanes, carry):
            kc = k_ref[pl.ds(start, ATT_TQ), ln].astype(BF16)
            vc = v_ref[pl.ds(start, ATT_TQ), ln].astype(BF16)
            s = lax.dot_general(q, kc, (((1,), (1,)), ((), ())), preferred_element_type=F32)
            s = jnp.where(cw > 0, s, -jnp.inf)
            m_new = jnp.maximum(m, jnp.max(s, axis=-1, keepdims=True))
            p = cw * jnp.exp(s - m_new)
            alpha = jnp.exp(m - m_new)
            l = alpha * l + jnp.sum(p, axis=-1, keepdims=True)
            acc = alpha * acc + jnp.dot(p.astype(BF16), vc, preferred_element_type=F32)
            out.append((m_new, l, acc))
        return tuple(out)

    init = tuple((jnp.full((ATT_TQ, 1), -jnp.inf, F32), jnp.zeros((ATT_TQ, 1), F32),
                  jnp.zeros((ATT_TQ, HEAD_DIM), F32)) for _ in range(ATT_HEADS))
    n_chunks = jnp.minimum(qb, ATT_CHUNKS - 1) + 1
    final = lax.fori_loop(0, n_chunks, chunk, init)
    for ln, (_, l, acc) in zip(lanes, final):
        o_ref[:, ln] = acc / l


def _prompt_attention(proj, batch, seq):
    qi = np.arange(ATT_TQ)[:, None]
    kj = np.arange(ATT_TQ)[None, :]
    table = np.stack([_branch_weight(qi - kj + ATT_TQ * c) for c in range(ATT_CHUNKS)])
    nqb = seq // ATT_TQ
    groups = N_HEADS // ATT_HEADS
    width = ATT_HEADS * HEAD_DIM
    return pl.pallas_call(
        _prompt_attn_kernel,
        grid=(batch, groups, nqb),
        in_specs=[
            pl.BlockSpec((ATT_TQ, width), lambda b, h, i: (b * nqb + i, h)),
            pl.BlockSpec((seq, width), lambda b, h, i: (b, groups + h)),
            pl.BlockSpec((seq, width), lambda b, h, i: (b, 2 * groups + h)),
            pl.BlockSpec((ATT_CHUNKS, ATT_TQ, ATT_TQ), lambda b, h, i: (0, 0, 0)),
        ],
        out_specs=pl.BlockSpec((ATT_TQ, width), lambda b, h, i: (b * nqb + i, h)),
        out_shape=jax.ShapeDtypeStruct((batch * seq, D_ATT), F32),
        compiler_params=_params(("parallel", "parallel", "parallel")),
        name="prompt_attn",
    )(proj, proj, proj, jnp.asarray(table))


SA_HEADS = 8
SA_NEAR = 512
LANE = 128


def _sample_attn_kernel(q_ref, kn_ref, vn_ref, *refs):
    t_new = q_ref.shape[0]
    kfar = refs[0:t_new]
    knear = refs[t_new]
    vfar = refs[t_new + 1:2 * t_new + 1]
    vnear = refs[2 * t_new + 1]
    c_ref, o_ref, kb_ref, vb_ref = refs[2 * t_new + 2:]
    n_keys = kb_ref.shape[0]

    def put(dst, off, src):
        rows = src.shape[0] * SA_HEADS
        dst[off:off + rows, :] = src[...].reshape(rows, HEAD_DIM).astype(BF16)
        return off + rows

    off = 0
    for k_src, v_src in list(zip(kfar, vfar)) + [(knear, vnear), (kn_ref, vn_ref)]:
        put(kb_ref, off, k_src)
        off = put(vb_ref, off, v_src)
    kb_ref[off:, :] = jnp.zeros((n_keys - off, HEAD_DIM), BF16)
    vb_ref[off:, :] = jnp.zeros((n_keys - off, HEAD_DIM), BF16)

    rows = t_new * SA_HEADS
    q = (q_ref[...].reshape(rows, HEAD_DIM) * (HEAD_DIM ** -0.5)).astype(BF16)
    s = lax.dot_general(q, kb_ref[...], (((1,), (1,)), ((), ())), preferred_element_type=F32)
    cw = c_ref[...]
    s = jnp.where(cw > 0, s, -jnp.inf)
    m = jnp.max(s, axis=-1, keepdims=True)
    p = cw * jnp.exp(s - m)
    l = jnp.sum(p, axis=-1, keepdims=True)
    o = jnp.dot(p.astype(BF16), vb_ref[...], preferred_element_type=F32) / l
    o_ref[...] = o.reshape(t_new, SA_HEADS, HEAD_DIM)


def _sample_attention(proj_s, cache_k, cache_v):
    bsz, t_new = proj_s.shape[:2]
    n_past = cache_k.shape[1]
    dil = BRANCHES[-1][1]
    assert n_past == MAX_WINDOW and n_past % dil == 0 and t_new <= dil
    assert all(w <= SA_NEAR for w, _ in BRANCHES[:-1])
    n_far = (n_past - SA_NEAR) // dil
    n_groups = n_past // dil
    n_keys = -(-((t_new * n_far + SA_NEAR + t_new) * SA_HEADS) // LANE) * LANE

    pos_list = [dil * np.arange(n_far) + (n_past + r) % dil for r in range(t_new)]
    pos_list += [n_past - SA_NEAR + np.arange(SA_NEAR), n_past + np.arange(t_new)]
    pos_used = np.repeat(np.concatenate(pos_list), SA_HEADS)
    pos = np.full((n_keys,), -1, np.int64)
    pos[:pos_used.size] = pos_used
    head = np.arange(n_keys) % SA_HEADS
    row = np.arange(t_new * SA_HEADS)
    dist = (n_past + row // SA_HEADS)[:, None] - pos[None, :]
    keep = (pos[None, :] >= 0) & (head[None, :] == (row % SA_HEADS)[:, None])
    table = np.where(keep, _branch_weight(dist), 0.0).astype(np.float32)

    def new_spec(which):
        return pl.BlockSpec((None, t_new, SA_HEADS, HEAD_DIM),
                            lambda b, g, which=which: (b, 0, which * (N_HEADS // SA_HEADS) + g, 0))

    def far_spec(r):
        res = (n_past + r) % dil
        return pl.BlockSpec((None, n_far, None, SA_HEADS, HEAD_DIM),
                            lambda b, g, res=res: (b, 0, res, g, 0))

    near_spec = pl.BlockSpec((None, SA_NEAR, SA_HEADS, HEAD_DIM),
                             lambda b, g: (b, n_past // SA_NEAR - 1, g, 0))
    ck_far = cache_k.reshape(bsz, n_groups, dil, N_HEADS, HEAD_DIM)
    cv_far = cache_v.reshape(bsz, n_groups, dil, N_HEADS, HEAD_DIM)
    in_specs = ([new_spec(0), new_spec(1), new_spec(2)]
                + [far_spec(r) for r in range(t_new)] + [near_spec]
                + [far_spec(r) for r in range(t_new)] + [near_spec]
                + [pl.BlockSpec((t_new * SA_HEADS, n_keys), lambda b, g: (0, 0))])
    args = ([proj_s] * 3 + [ck_far] * t_new + [cache_k] + [cv_far] * t_new + [cache_v]
            + [jnp.asarray(table)])
    return pl.pallas_call(
        _sample_attn_kernel,
        grid=(bsz, N_HEADS // SA_HEADS),
        in_specs=in_specs,
        out_specs=pl.BlockSpec((None, t_new, SA_HEADS, HEAD_DIM), lambda b, g: (b, 0, g, 0)),
        out_shape=jax.ShapeDtypeStruct((bsz, t_new, N_HEADS, HEAD_DIM), F32),
        scratch_shapes=[pltpu.VMEM((n_keys, HEAD_DIM), BF16),
                        pltpu.VMEM((n_keys, HEAD_DIM), BF16)],
        compiler_params=_params(("parallel", "parallel")),
        name="sample_attn",
    )(*args)


CONV_TC = 128
CONV_HALO = 8


def _prompt_conv_kernel(b_ref, c_ref, x_ref, w_ref, o_ref, last_ref, u_ref):
    seq = b_ref.shape[0]
    u = c_ref[...] * x_ref[...]
    u_ref[0:CONV_HALO, :] = jnp.zeros((CONV_HALO, CONV_TC), F32)
    u_ref[CONV_HALO:, :] = u
    w = w_ref[...]
    acc = w[CONV_WIDTH - 1:CONV_WIDTH, :] * u
    for tap in range(CONV_WIDTH - 1):
        shift = CONV_WIDTH - 1 - tap
        acc = acc + w[tap:tap + 1, :] * u_ref[CONV_HALO - shift:CONV_HALO - shift + seq, :]
    o_ref[...] = b_ref[...] * acc
    last_ref[0] = u_ref[CONV_HALO + seq - (CONV_WIDTH - 1):, :]


def _prompt_conv(proj, conv_w, batch, seq):
    base = 3 * D_ATT // CONV_TC
    per = D_CONV // CONV_TC
    return pl.pallas_call(
        _prompt_conv_kernel,
        grid=(batch, per),
        in_specs=[pl.BlockSpec((seq, CONV_TC), lambda b, j: (b, base + j)),
                  pl.BlockSpec((seq, CONV_TC), lambda b, j: (b, base + per + j)),
                  pl.BlockSpec((seq, CONV_TC), lambda b, j: (b, base + 2 * per + j)),
                  pl.BlockSpec((CONV_WIDTH, CONV_TC), lambda b, j: (0, j))],
        out_specs=[pl.BlockSpec((seq, CONV_TC), lambda b, j: (b, j)),
                   pl.BlockSpec((1, CONV_WIDTH - 1, CONV_TC), lambda b, j: (b, 0, j))],
        out_shape=[jax.ShapeDtypeStruct((batch * seq, D_CONV), F32),
                   jax.ShapeDtypeStruct((batch, CONV_WIDTH - 1, D_CONV), F32)],
        scratch_shapes=[pltpu.VMEM((CONV_HALO + seq, CONV_TC), F32)],
        compiler_params=_params(("parallel", "parallel")),
        name="prompt_conv",
    )(proj, proj, proj, conv_w)


def _sample_conv_kernel(t_new, g_ref, st_ref, w_ref, o_ref, u_ref):
    w = w_ref[...]
    ext = [st_ref[:, i * D_CONV:(i + 1) * D_CONV] for i in range(CONV_WIDTH - 1)]
    for t in range(t_new):
        base = t * 3 * D_CONV
        u = g_ref[:, base + D_CONV:base + 2 * D_CONV] * g_ref[:, base + 2 * D_CONV:base + 3 * D_CONV]
        u_ref[:, t * D_CONV:(t + 1) * D_CONV] = u
        ext.append(u)
    for t in range(t_new):
        acc = w[0:1, :] * ext[t]
        for tap in range(1, CONV_WIDTH):
            acc = acc + w[tap:tap + 1, :] * ext[t + tap]
        o_ref[:, t * D_CONV:(t + 1) * D_CONV] = g_ref[:, t * 3 * D_CONV:t * 3 * D_CONV + D_CONV] * acc


def _sample_conv(gates, state, conv_w):
    bsz = gates.shape[0]
    t_new = gates.shape[1] // (3 * D_CONV)
    full = lambda a: pl.BlockSpec(a.shape, lambda i: (0,) * a.ndim)
    out = jax.ShapeDtypeStruct((bsz, t_new * D_CONV), F32)
    return pl.pallas_call(
        functools.partial(_sample_conv_kernel, t_new),
        grid=(1,),
        in_specs=[full(gates), full(state), full(conv_w)],
        out_specs=[pl.BlockSpec(out.shape, lambda i: (0, 0))] * 2,
        out_shape=[out, out],
        compiler_params=_params(("arbitrary",)),
        name="sample_conv",
    )(gates, state, conv_w)


def _mixer_norm_kernel(a_ref, c_ref, ga_ref, gc_ref, o_ref):
    o_ref[:, :D_ATT] = (_rms(a_ref[...]) * ga_ref[...]).astype(o_ref.dtype)
    o_ref[:, D_ATT:] = (_rms(c_ref[...]) * gc_ref[...]).astype(o_ref.dtype)


def _mixer_norm(attn, conv, g_attn, g_conv, tm):
    n = attn.shape[0]
    return pl.pallas_call(
        _mixer_norm_kernel,
        grid=(n // tm,),
        in_specs=[pl.BlockSpec((tm, D_ATT), lambda i: (i, 0)),
                  pl.BlockSpec((tm, D_CONV), lambda i: (i, 0)),
                  pl.BlockSpec((1, D_ATT), lambda i: (0, 0)),
                  pl.BlockSpec((1, D_CONV), lambda i: (0, 0))],
        out_specs=pl.BlockSpec((tm, D_MODEL), lambda i: (i, 0)),
        out_shape=jax.ShapeDtypeStruct((n, D_MODEL), BF16),
        compiler_params=_params(("parallel",)),
        name="mixer_norm",
    )(attn, conv, g_attn.reshape(1, D_ATT), g_conv.reshape(1, D_CONV))


def _router_kernel(h_ref, g_ref, wr_ref, br_ref, xn_ref, e_ref, gate_ref):
    xn = _rms(h_ref[...]) * g_ref[...]
    xn_ref[...] = xn.astype(xn_ref.dtype)
    logits = jnp.dot(xn, wr_ref[...], preferred_element_type=F32,
                     precision=lax.Precision.HIGHEST) + br_ref[...]
    lane = lax.broadcasted_iota(jnp.int32, logits.shape, 1)
    vals = []
    for k in range(TOP_K):
        m = jnp.max(logits, axis=-1, keepdims=True)
        idx = jnp.min(jnp.where(logits == m, lane, N_EXPERTS), axis=-1, keepdims=True)
        e_ref[:, k:k + 1] = idx
        vals.append(m)
        logits = jnp.where(lane == idx, -jnp.inf, logits)
    ex = [jnp.exp(v - vals[0]) for v in vals]
    den = ex[0]
    for k in range(1, TOP_K):
        den = den + ex[k]
    for k in range(TOP_K):
        gate_ref[:, k:k + 1] = ex[k] / den


def _router(h, g, w_router, b_router, tm):
    n = h.shape[0]
    return pl.pallas_call(
        _router_kernel,
        grid=(n // tm,),
        in_specs=[pl.BlockSpec((tm, D_MODEL), lambda i: (i, 0)),
                  pl.BlockSpec((1, D_MODEL), lambda i: (0, 0)),
                  pl.BlockSpec((D_MODEL, N_EXPERTS), lambda i: (0, 0)),
                  pl.BlockSpec((1, N_EXPERTS), lambda i: (0, 0))],
        out_specs=[pl.BlockSpec((tm, D_MODEL), lambda i: (i, 0)),
                   pl.BlockSpec((tm, TOP_K), lambda i: (i, 0)),
                   pl.BlockSpec((tm, TOP_K), lambda i: (i, 0))],
        out_shape=[jax.ShapeDtypeStruct((n, D_MODEL), BF16),
                   jax.ShapeDtypeStruct((n, TOP_K), jnp.int32),
                   jax.ShapeDtypeStruct((n, TOP_K), F32)],
        compiler_params=_params(("parallel",)),
        name="router",
    )(h, g.reshape(1, D_MODEL), w_router, b_router.reshape(1, N_EXPERTS))


MOE_TM = 576
MOE_TB = 1024
MOE_KQ = D_MODEL // MOE_TB
MOE_NG = D_FF // MOE_TB
MOE_NO = D_MODEL // MOE_TB
MOE_P1 = MOE_NG * MOE_KQ
MOE_P2 = MOE_NO * MOE_KQ


def _experts_kernel(te_ref, tv_ref, *refs):
    x = refs[0:MOE_KQ]
    (wg_ref, wu_ref, bg_ref, bu_ref, wd_ref, bd_ref, gate_ref, o_ref,
     g_ref, u_ref, h_ref) = refs[MOE_KQ:]
    i = pl.program_id(0)
    s = pl.program_id(1)
    valid = tv_ref[i] > 0
    first = s < MOE_P1
    kq = s % MOE_KQ

    for q in range(MOE_KQ):
        @pl.when(jnp.logical_and(jnp.logical_and(valid, first), kq == q))
        def _():
            xq = x[q][...]
            pg = jnp.dot(xq, wg_ref[...].astype(BF16), preferred_element_type=F32)
            pu = jnp.dot(xq, wu_ref[...].astype(BF16), preferred_element_type=F32)
            if q == 0:
                g_ref[...] = pg
                u_ref[...] = pu
            else:
                g_ref[...] += pg
                u_ref[...] += pu

    @pl.when(jnp.logical_and(jnp.logical_and(valid, first), kq == MOE_KQ - 1))
    def _():
        glu = jnp.minimum(g_ref[...] + bg_ref[...], SWIGLU_LIMIT)
        lin = jnp.clip(u_ref[...] + bu_ref[...], -SWIGLU_LIMIT, SWIGLU_LIMIT)
        h = glu * jax.nn.sigmoid(SWIGLU_ALPHA * glu) * (lin + 1.0)
        h_ref[s // MOE_KQ] = h.astype(BF16)

    second = jnp.logical_and(valid, jnp.logical_not(first))
    for q in range(MOE_KQ):
        @pl.when(jnp.logical_and(second, kq == q))
        def _():
            y = jnp.dot(h_ref[q], wd_ref[...].astype(BF16), preferred_element_type=F32)
            if q == 0:
                o_ref[...] = y + bd_ref[...]
            else:
                o_ref[...] += y

    @pl.when(jnp.logical_and(second, kq == MOE_KQ - 1))
    def _():
        o_ref[...] = o_ref[...] * gate_ref[...]

    @pl.when(jnp.logical_and(jnp.logical_not(valid), jnp.logical_not(first)))
    def _():
        o_ref[...] = jnp.zeros_like(o_ref)


def _experts(xs, slot_gate, tile_expert, tile_valid, w_gate_up, b_gate_up, w_down, b_down):
    n_slots = xs.shape[0]
    n_tiles = n_slots // MOE_TM

    def p1(s, tv, i):
        s1 = jnp.where(tv[i] > 0, jnp.minimum(s, MOE_P1 - 1), MOE_P1 - 1)
        return s1 % MOE_KQ, s1 // MOE_KQ

    def p2(s):
        s2 = jnp.clip(s - MOE_P1, 0, MOE_P2 - 1)
        return s2 % MOE_KQ, s2 // MOE_KQ

    def p2_w(s, tv, i):
        s2 = jnp.where(tv[i] > 0, jnp.clip(s - MOE_P1, 0, MOE_P2 - 1), MOE_P2 - 1)
        return s2 % MOE_KQ, s2 // MOE_KQ

    def gate_up_spec(col0):
        def index(i, s, te, tv):
            piece, group = p1(s, tv, i)
            return te[i], piece, col0 + group
        return pl.BlockSpec((None, MOE_TB, MOE_TB), index)

    def gate_up_bias_spec(col0):
        return pl.BlockSpec((None, 1, MOE_TB),
                            lambda i, s, te, tv: (te[i], 0, col0 + p1(s, tv, i)[1]))

    def down_index(i, s, te, tv):
        piece, group = p2_w(s, tv, i)
        return te[i], piece, group

    grid_spec = pltpu.PrefetchScalarGridSpec(
        num_scalar_prefetch=2,
        grid=(n_tiles, MOE_P1 + MOE_P2),
        in_specs=(
            [pl.BlockSpec((MOE_TM, MOE_TB), lambda i, s, te, tv, q=q: (i, q),
                          pipeline_mode=pl.Buffered(1)) for q in range(MOE_KQ)]
            + [gate_up_spec(0), gate_up_spec(MOE_NG),
               gate_up_bias_spec(0), gate_up_bias_spec(MOE_NG),
               pl.BlockSpec((None, MOE_TB, MOE_TB), down_index),
               pl.BlockSpec((None, 1, MOE_TB), lambda i, s, te, tv: (te[i], 0, p2_w(s, tv, i)[1])),
               pl.BlockSpec((MOE_TM, 1), lambda i, s, te, tv: (i, 0))]
        ),
        out_specs=pl.BlockSpec((MOE_TM, MOE_TB), lambda i, s, te, tv: (i, p2(s)[1])),
        scratch_shapes=[pltpu.VMEM((MOE_TM, MOE_TB), F32),
                        pltpu.VMEM((MOE_TM, MOE_TB), F32),
                        pltpu.VMEM((MOE_NG, MOE_TM, MOE_TB), BF16)],
    )
    b_gu = b_gate_up.reshape(N_EXPERTS, 1, 2 * D_FF)
    return pl.pallas_call(
        _experts_kernel,
        grid_spec=grid_spec,
        out_shape=jax.ShapeDtypeStruct((n_slots, D_MODEL), F32),
        compiler_params=_params(("arbitrary", "arbitrary")),
        name="experts",
    )(tile_expert, tile_valid, *([xs] * MOE_KQ), w_gate_up, w_gate_up, b_gu, b_gu,
      w_down, b_down.reshape(N_EXPERTS, 1, D_MODEL), slot_gate)


def _dispatch(top_e, gate):
    n = top_e.shape[0]
    n_assign = n * TOP_K
    n_tiles = -(-n_assign // MOE_TM) + N_EXPERTS
    n_slots = n_tiles * MOE_TM
    flat_e = top_e.reshape(-1)
    onehot = (flat_e[:, None] == jnp.arange(N_EXPERTS, dtype=jnp.int32)[None, :]).astype(jnp.int32)
    csum = jnp.cumsum(onehot, axis=0)
    rank = jnp.sum(csum * onehot, axis=1) - 1
    counts = csum[-1]
    tiles = (counts + MOE_TM - 1) // MOE_TM
    tile_end = jnp.cumsum(tiles)
    pad_start = (tile_end - tiles) * MOE_TM
    dest = pad_start[flat_e] + rank
    tok = jnp.arange(n_assign, dtype=jnp.int32) // TOP_K
    gate_bits = lax.bitcast_convert_type(gate.reshape(-1), jnp.int32)
    pad_tok = jnp.arange(n_slots, dtype=jnp.int32) % n
    slots = jnp.stack([pad_tok, jnp.zeros((n_slots,), jnp.int32)], axis=1)
    slots = slots.at[dest].set(jnp.stack([tok, gate_bits], axis=1))
    slot_tok = slots[:, 0]
    slot_gate = lax.bitcast_convert_type(slots[:, 1], F32)
    tile_ids = jnp.arange(n_tiles, dtype=jnp.int32)
    tile_valid = (tile_ids < tile_end[-1]).astype(jnp.int32)
    last = jnp.maximum(tile_end[-1] - 1, 0)
    tile_expert = jnp.sum(
        (tile_end[None, :] <= jnp.minimum(tile_ids, last)[:, None]).astype(jnp.int32), axis=1)
    tile_expert = jnp.minimum(tile_expert, N_EXPERTS - 1).astype(jnp.int32)
    return slot_tok, slot_gate.reshape(n_slots, 1), tile_expert, tile_valid, dest.reshape(n, TOP_K)


def _final_kernel(h_ref, *refs):
    y_refs, g_ref, o_ref = refs[:TOP_K], refs[TOP_K], refs[TOP_K + 1]
    h = h_ref[...]
    for y_ref in y_refs:
        h = h + y_ref[...]
    o_ref[...] = _rms(h) * g_ref[...]


def _final(h, y_rows, g, row0, n_rows, tm):
    n = h.shape[0]
    blk0 = row0 // tm
    y_specs = [pl.BlockSpec((tm, D_MODEL), lambda i, k=k: (k * (n // tm) + blk0 + i, 0))
               for k in range(TOP_K)]
    return pl.pallas_call(
        _final_kernel,
        grid=(n_rows // tm,),
        in_specs=[pl.BlockSpec((tm, D_MODEL), lambda i: (blk0 + i, 0))] + y_specs
                 + [pl.BlockSpec((1, D_MODEL), lambda i: (0, 0))],
        out_specs=pl.BlockSpec((tm, D_MODEL), lambda i: (i, 0)),
        out_shape=jax.ShapeDtypeStruct((n_rows, D_MODEL), F32),
        compiler_params=_params(("parallel",)),
        name="final_norm",
    )(h, *([y_rows] * TOP_K), g.reshape(1, D_MODEL))


def kernel(x_prompt, x_sample, cache_k, cache_v, state_conv, norm_mix, w_in, conv_w,
           norm_attn_out, norm_conv_out, w_out, norm_ffn, w_router, b_router,
           w_gate_up, b_gate_up, w_down, b_down, norm_final):
    batch, seq, _ = x_prompt.shape
    bsz, t_new, _ = x_sample.shape
    depth = w_in.shape[0]
    assert depth == 1 and t_new >= CONV_WIDTH - 1
    n_p, n_s = batch * seq, bsz * t_new
    n = n_p + n_s
    layer = 0

    x_all = jnp.concatenate([x_prompt.reshape(n_p, D_MODEL), x_sample.reshape(n_s, D_MODEL)], axis=0)
    xn = _norm_cast(x_all, norm_mix[layer], 512)
    proj = _matmul(xn, w_in[layer].astype(BF16), None, 512, 1024)
    proj_s = lax.slice(proj, (n_p, 0), (n, D_IN))
    proj_s4 = proj_s.reshape(bsz, t_new, D_IN // HEAD_DIM, HEAD_DIM)

    attn_p = _prompt_attention(proj, batch, seq)
    attn_s = _sample_attention(proj_s4, cache_k[layer], cache_v[layer])
    conv_p, u_last_p = _prompt_conv(proj, conv_w[layer], batch, seq)
    conv_s, u_s = _sample_conv(proj_s[:, 3 * D_ATT:].reshape(bsz, t_new * 3 * D_CONV),
                               state_conv[layer].reshape(bsz, (CONV_WIDTH - 1) * D_CONV),
                               conv_w[layer])

    attn = jnp.concatenate([attn_p, attn_s.reshape(n_s, D_ATT)], axis=0)
    conv = jnp.concatenate([conv_p, conv_s.reshape(n_s, D_CONV)], axis=0)
    cat = _mixer_norm(attn, conv, norm_attn_out[layer], norm_conv_out[layer], 512)
    h1 = _matmul(cat, w_out[layer].astype(BF16), x_all, 512, 1024)

    xn2, top_e, gate = _router(h1, norm_ffn[layer], w_router[layer], b_router[layer], 512)
    slot_tok, slot_gate, tile_expert, tile_valid, dest = _dispatch(top_e, gate)
    xs = xn2[slot_tok]
    ys = _experts(xs, slot_gate, tile_expert, tile_valid, w_gate_up[layer], b_gate_up[layer],
                  w_down[layer], b_down[layer])
    y_rows = ys[dest.T.reshape(-1)]
    y_prompt = _final(h1, y_rows, norm_final, 0, n_p, 128).reshape(batch, seq, D_MODEL)
    y_sample = _final(h1, y_rows, norm_final, n_p, n_s, 128).reshape(bsz, t_new, D_MODEL)

    n_keep = min(MAX_WINDOW, seq)

    def prompt_window(col0):
        parts = [lax.slice(proj, ((b + 1) * seq - n_keep, col0), ((b + 1) * seq, col0 + D_ATT))
                 for b in range(batch)]
        return jnp.stack(parts).reshape(1, batch, n_keep, N_HEADS, HEAD_DIM)

    k_prompt_win = prompt_window(D_ATT)
    v_prompt_win = prompt_window(2 * D_ATT)
    conv_prompt = u_last_p[None]
    k_sample_new = proj_s4[:, :, N_HEADS:2 * N_HEADS][None]
    v_sample_new = proj_s4[:, :, 2 * N_HEADS:3 * N_HEADS][None]
    conv_sample = u_s.reshape(bsz, t_new, D_CONV)[:, t_new - (CONV_WIDTH - 1):][None]
    return (y_prompt, y_sample, k_prompt_win, v_prompt_win, conv_prompt,
            k_sample_new, v_sample_new, conv_sample)
```

```python
import functools

import numpy as np
import jax
import jax.numpy as jnp
from jax import lax
from jax.experimental import pallas as pl
from jax.experimental.pallas import tpu as pltpu

D_MODEL = 4096
HEAD_DIM = 128
N_HEADS = 24
D_ATT = N_HEADS * HEAD_DIM
D_CONV = D_MODEL - D_ATT
CONV_WIDTH = 3
BRANCHES = ((128, 1), (512, 4), (2048, 16))
MAX_WINDOW = 2048
D_IN = 3 * D_ATT + 3 * D_CONV
N_EXPERTS = 32
TOP_K = 4
D_FF = D_MODEL
SWIGLU_ALPHA = 1.702
SWIGLU_LIMIT = 7.0
RMS_EPS = 1e-5

F32 = jnp.float32
BF16 = jnp.bfloat16

VMEM_LIMIT_BYTES = 56 * 1024 * 1024


def _params(semantics):
    return pltpu.CompilerParams(dimension_semantics=semantics,
                                vmem_limit_bytes=VMEM_LIMIT_BYTES)


def _branch_weight(dist):
    dist = np.asarray(dist)
    c = np.zeros(dist.shape, np.float32)
    for window, dil in BRANCHES:
        c += ((dist >= 0) & (dist <= window) & (dist % dil == 0)).astype(np.float32)
    return c


def _rms(x):
    return x * lax.rsqrt(jnp.mean(x * x, axis=-1, keepdims=True) + RMS_EPS)


def _norm_cast_kernel(x_ref, g_ref, o_ref):
    o_ref[...] = (_rms(x_ref[...]) * g_ref[...]).astype(o_ref.dtype)


def _norm_cast(x, g, tm):
    n, d = x.shape
    return pl.pallas_call(
        _norm_cast_kernel,
        grid=(n // tm,),
        in_specs=[pl.BlockSpec((tm, d), lambda i: (i, 0)),
                  pl.BlockSpec((1, d), lambda i: (0, 0))],
        out_specs=pl.BlockSpec((tm, d), lambda i: (i, 0)),
        out_shape=jax.ShapeDtypeStruct((n, d), BF16),
        compiler_params=_params(("parallel",)),
        name="norm_cast",
    )(x, g.reshape(1, d))


def _mm_kernel(x_ref, w_ref, o_ref):
    o_ref[...] = jnp.dot(x_ref[...], w_ref[...], preferred_element_type=F32)


def _mm_res_kernel(x_ref, w_ref, r_ref, o_ref):
    o_ref[...] = r_ref[...] + jnp.dot(x_ref[...], w_ref[...], preferred_element_type=F32)


def _matmul(x, w, residual, tm, tn):
    n, k = x.shape
    m = w.shape[1]
    in_specs = [pl.BlockSpec((tm, k), lambda j, i: (i, 0)),
                pl.BlockSpec((k, tn), lambda j, i: (0, j))]
    args = [x, w]
    body = _mm_kernel
    if residual is not None:
        in_specs.append(pl.BlockSpec((tm, tn), lambda j, i: (i, j)))
        args.append(residual)
        body = _mm_res_kernel
    return pl.pallas_call(
        body,
        grid=(m // tn, n // tm),
        in_specs=in_specs,
        out_specs=pl.BlockSpec((tm, tn), lambda j, i: (i, j)),
        out_shape=jax.ShapeDtypeStruct((n, m), F32),
        compiler_params=_params(("parallel", "parallel")),
        name="proj_res" if residual is not None else "proj",
    )(*args)


ATT_TQ = 256
ATT_HEADS = 4
ATT_CHUNKS = MAX_WINDOW // ATT_TQ + 1


def _prompt_attn_kernel(q_ref, k_ref, v_ref, c_ref, o_ref):
    qb = pl.program_id(2)
    lanes = [slice(h * HEAD_DIM, (h + 1) * HEAD_DIM) for h in range(ATT_HEADS)]
    qs = [(q_ref[:, ln] * (HEAD_DIM ** -0.5)).astype(BF16) for ln in lanes]

    def chunk(c, carry):
        start = pl.multiple_of((qb - c) * ATT_TQ, ATT_TQ)
        cw = c_ref[c]
        out = []
        for q, ln, (m, l, acc) in zip(qs, lanes, carry):
            kc = k_ref[pl.ds(start, ATT_TQ), ln].astype(BF16)
            vc = v_ref[pl.ds(start, ATT_TQ), ln].astype(BF16)
            s = lax.dot_general(q, kc, (((1,), (1,)), ((), ())), preferred_element_type=F32)
            s = jnp.where(cw > 0, s, -jnp.inf)
            m_new = jnp.maximum(m, jnp.max(s, axis=-1, keepdims=True))
            p = cw * jnp.exp(s - m_new)
            alpha = jnp.exp(m - m_new)
            l = alpha * l + jnp.sum(p, axis=-1, keepdims=True)
            acc = alpha * acc + jnp.dot(p.astype(BF16), vc, preferred_element_type=F32)
            out.append((m_new, l, acc))
        return tuple(out)

    init = tuple((jnp.full((ATT_TQ, 1), -jnp.inf, F32), jnp.zeros((ATT_TQ, 1), F32),
                  jnp.zeros((ATT_TQ, HEAD_DIM), F32)) for _ in range(ATT_HEADS))
    n_chunks = jnp.minimum(qb, ATT_CHUNKS - 1) + 1
    final = lax.fori_loop(0, n_chunks, chunk, init)
    for ln, (_, l, acc) in zip(lanes, final):
        o_ref[:, ln] = acc / l


def _prompt_attention(proj, batch, seq):
    qi = np.arange(ATT_TQ)[:, None]
    kj = np.arange(ATT_TQ)[None, :]
    table = np.stack([_branch_weight(qi - kj + ATT_TQ * c) for c in range(ATT_CHUNKS)])
    nqb = seq // ATT_TQ
    groups = N_HEADS // ATT_HEADS
    width = ATT_HEADS * HEAD_DIM
    return pl.pallas_call(
        _prompt_attn_kernel,
        grid=(batch, groups, nqb),
        in_specs=[
            pl.BlockSpec((ATT_TQ, width), lambda b, h, i: (b * nqb + i, h)),
            pl.BlockSpec((seq, width), lambda b, h, i: (b, groups + h)),
            pl.BlockSpec((seq, width), lambda b, h, i: (b, 2 * groups + h)),
            pl.BlockSpec((ATT_CHUNKS, ATT_TQ, ATT_TQ), lambda b, h, i: (0, 0, 0)),
        ],
        out_specs=pl.BlockSpec((ATT_TQ, width), lambda b, h, i: (b * nqb + i, h)),
        out_shape=jax.ShapeDtypeStruct((batch * seq, D_ATT), F32),
        compiler_params=_params(("parallel", "parallel", "parallel")),
        name="prompt_attn",
    )(proj, proj, proj, jnp.asarray(table))


SA_HEADS = 8
SA_NEAR = 512
LANE = 128


def _sample_attn_kernel(q_ref, kn_ref, vn_ref, *refs):
    t_new = q_ref.shape[0]
    kfar = refs[0:t_new]
    knear = refs[t_new]
    vfar = refs[t_new + 1:2 * t_new + 1]
    vnear = refs[2 * t_new + 1]
    c_ref, o_ref, kb_ref, vb_ref = refs[2 * t_new + 2:]
    n_keys = kb_ref.shape[0]

    def put(dst, off, src):
        rows = src.shape[0] * SA_HEADS
        dst[off:off + rows, :] = src[...].reshape(rows, HEAD_DIM).astype(BF16)
        return off + rows

    off = 0
    for k_src, v_src in list(zip(kfar, vfar)) + [(knear, vnear), (kn_ref, vn_ref)]:
        put(kb_ref, off, k_src)
        off = put(vb_ref, off, v_src)
    kb_ref[off:, :] = jnp.zeros((n_keys - off, HEAD_DIM), BF16)
    vb_ref[off:, :] = jnp.zeros((n_keys - off, HEAD_DIM), BF16)

    rows = t_new * SA_HEADS
    q = (q_ref[...].reshape(rows, HEAD_DIM) * (HEAD_DIM ** -0.5)).astype(BF16)
    s = lax.dot_general(q, kb_ref[...], (((1,), (1,)), ((), ())), preferred_element_type=F32)
    cw = c_ref[...]
    s = jnp.where(cw > 0, s, -jnp.inf)
    m = jnp.max(s, axis=-1, keepdims=True)
    p = cw * jnp.exp(s - m)
    l = jnp.sum(p, axis=-1, keepdims=True)
    o = jnp.dot(p.astype(BF16), vb_ref[...], preferred_element_type=F32) / l
    o_ref[...] = o.reshape(t_new, SA_HEADS, HEAD_DIM)


def _sample_attention(proj_s, cache_k, cache_v):
    bsz, t_new = proj_s.shape[:2]
    n_past = cache_k.shape[1]
    dil = BRANCHES[-1][1]
    assert n_past == MAX_WINDOW and n_past % dil == 0 and t_new <= dil
    assert all(w <= SA_NEAR for w, _ in BRANCHES[:-1])
    n_far = (n_past - SA_NEAR) // dil
    n_groups = n_past // dil
    n_keys = -(-((t_new * n_far + SA_NEAR + t_new) * SA_HEADS) // LANE) * LANE

    pos_list = [dil * np.arange(n_far) + (n_past + r) % dil for r in range(t_new)]
    pos_list += [n_past - SA_NEAR + np.arange(SA_NEAR), n_past + np.arange(t_new)]
    pos_used = np.repeat(np.concatenate(pos_list), SA_HEADS)
    pos = np.full((n_keys,), -1, np.int64)
    pos[:pos_used.size] = pos_used
    head = np.arange(n_keys) % SA_HEADS
    row = np.arange(t_new * SA_HEADS)
    dist = (n_past + row // SA_HEADS)[:, None] - pos[None, :]
    keep = (pos[None, :] >= 0) & (head[None, :] == (row % SA_HEADS)[:, None])
    table = np.where(keep, _branch_weight(dist), 0.0).astype(np.float32)

    def new_spec(which):
        return pl.BlockSpec((None, t_new, SA_HEADS, HEAD_DIM),
                            lambda b, g, which=which: (b, 0, which * (N_HEADS // SA_HEADS) + g, 0))

    def far_spec(r):
        res = (n_past + r) % dil
        return pl.BlockSpec((None, n_far, None, SA_HEADS, HEAD_DIM),
                            lambda b, g, res=res: (b, 0, res, g, 0))

    near_spec = pl.BlockSpec((None, SA_NEAR, SA_HEADS, HEAD_DIM),
                             lambda b, g: (b, n_past // SA_NEAR - 1, g, 0))
    ck_far = cache_k.reshape(bsz, n_groups, dil, N_HEADS, HEAD_DIM)
    cv_far = cache_v.reshape(bsz, n_groups, dil, N_HEADS, HEAD_DIM)
    in_specs = ([new_spec(0), new_spec(1), new_spec(2)]
                + [far_spec(r) for r in range(t_new)] + [near_spec]
                + [far_spec(r) for r in range(t_new)] + [near_spec]
                + [pl.BlockSpec((t_new * SA_HEADS, n_keys), lambda b, g: (0, 0))])
    args = ([proj_s] * 3 + [ck_far] * t_new + [cache_k] + [cv_far] * t_new + [cache_v]
            + [jnp.asarray(table)])
    return pl.pallas_call(
        _sample_attn_kernel,
        grid=(bsz, N_HEADS // SA_HEADS),
        in_specs=in_specs,
        out_specs=pl.BlockSpec((None, t_new, SA_HEADS, HEAD_DIM), lambda b, g: (b, 0, g, 0)),
        out_shape=jax.ShapeDtypeStruct((bsz, t_new, N_HEADS, HEAD_DIM), F32),
        scratch_shapes=[pltpu.VMEM((n_keys, HEAD_DIM), BF16),
                        pltpu.VMEM((n_keys, HEAD_DIM), BF16)],
        compiler_params=_params(("parallel", "parallel")),
        name="sample_attn",
    )(*args)


CONV_TC = 128
CONV_HALO = 8


def _prompt_conv_kernel(b_ref, c_ref, x_ref, w_ref, o_ref, last_ref, u_ref):
    seq = b_ref.shape[0]
    u = c_ref[...] * x_ref[...]
    u_ref[0:CONV_HALO, :] = jnp.zeros((CONV_HALO, CONV_TC), F32)
    u_ref[CONV_HALO:, :] = u
    w = w_ref[...]
    acc = w[CONV_WIDTH - 1:CONV_WIDTH, :] * u
    for tap in range(CONV_WIDTH - 1):
        shift = CONV_WIDTH - 1 - tap
        acc = acc + w[tap:tap + 1, :] * u_ref[CONV_HALO - shift:CONV_HALO - shift + seq, :]
    o_ref[...] = b_ref[...] * acc
    last_ref[0] = u_ref[CONV_HALO + seq - (CONV_WIDTH - 1):, :]


def _prompt_conv(proj, conv_w, batch, seq):
    base = 3 * D_ATT // CONV_TC
    per = D_CONV // CONV_TC
    return pl.pallas_call(
        _prompt_conv_kernel,
        grid=(batch, per),
        in_specs=[pl.BlockSpec((seq, CONV_TC), lambda b, j: (b, base + j)),
                  pl.BlockSpec((seq, CONV_TC), lambda b, j: (b, base + per + j)),
                  pl.BlockSpec((seq, CONV_TC), lambda b, j: (b, base + 2 * per + j)),
                  pl.BlockSpec((CONV_WIDTH, CONV_TC), lambda b, j: (0, j))],
        out_specs=[pl.BlockSpec((seq, CONV_TC), lambda b, j: (b, j)),
                   pl.BlockSpec((1, CONV_WIDTH - 1, CONV_TC), lambda b, j: (b, 0, j))],
        out_shape=[jax.ShapeDtypeStruct((batch * seq, D_CONV), F32),
                   jax.ShapeDtypeStruct((batch, CONV_WIDTH - 1, D_CONV), F32)],
        scratch_shapes=[pltpu.VMEM((CONV_HALO + seq, CONV_TC), F32)],
        compiler_params=_params(("parallel", "parallel")),
        name="prompt_conv",
    )(proj, proj, proj, conv_w)


def _sample_conv_kernel(t_new, g_ref, st_ref, w_ref, o_ref, u_ref):
    w = w_ref[...]
    ext = [st_ref[:, i * D_CONV:(i + 1) * D_CONV] for i in range(CONV_WIDTH - 1)]
    for t in range(t_new):
        base = t * 3 * D_CONV
        u = g_ref[:, base + D_CONV:base + 2 * D_CONV] * g_ref[:, base + 2 * D_CONV:base + 3 * D_CONV]
        u_ref[:, t * D_CONV:(t + 1) * D_CONV] = u
        ext.append(u)
    for t in range(t_new):
        acc = w[0:1, :] * ext[t]
        for tap in range(1, CONV_WIDTH):
            acc = acc + w[tap:tap + 1, :] * ext[t + tap]
        o_ref[:, t * D_CONV:(t + 1) * D_CONV] = g_ref[:, t * 3 * D_CONV:t * 3 * D_CONV + D_CONV] * acc


def _sample_conv(gates, state, conv_w):
    bsz = gates.shape[0]
    t_new = gates.shape[1] // (3 * D_CONV)
    full = lambda a: pl.BlockSpec(a.shape, lambda i: (0,) * a.ndim)
    out = jax.ShapeDtypeStruct((bsz, t_new * D_CONV), F32)
    return pl.pallas_call(
        functools.partial(_sample_conv_kernel, t_new),
        grid=(1,),
        in_specs=[full(gates), full(state), full(conv_w)],
        out_specs=[pl.BlockSpec(out.shape, lambda i: (0, 0))] * 2,
        out_shape=[out, out],
        compiler_params=_params(("arbitrary",)),
        name="sample_conv",
    )(gates, state, conv_w)


def _mixer_norm_kernel(a_ref, c_ref, ga_ref, gc_ref, o_ref):
    o_ref[:, :D_ATT] = (_rms(a_ref[...]) * ga_ref[...]).astype(o_ref.dtype)
    o_ref[:, D_ATT:] = (_rms(c_ref[...]) * gc_ref[...]).astype(o_ref.dtype)


def _mixer_norm(attn, conv, g_attn, g_conv, tm):
    n = attn.shape[0]
    return pl.pallas_call(
        _mixer_norm_kernel,
        grid=(n // tm,),
        in_specs=[pl.BlockSpec((tm, D_ATT), lambda i: (i, 0)),
                  pl.BlockSpec((tm, D_CONV), lambda i: (i, 0)),
                  pl.BlockSpec((1, D_ATT), lambda i: (0, 0)),
                  pl.BlockSpec((1, D_CONV), lambda i: (0, 0))],
        out_specs=pl.BlockSpec((tm, D_MODEL), lambda i: (i, 0)),
        out_shape=jax.ShapeDtypeStruct((n, D_MODEL), BF16),
        compiler_params=_params(("parallel",)),
        name="mixer_norm",
    )(attn, conv, g_attn.reshape(1, D_ATT), g_conv.reshape(1, D_CONV))


def _router_kernel(h_ref, g_ref, wr_ref, br_ref, xn_ref, e_ref, gate_ref):
    xn = _rms(h_ref[...]) * g_ref[...]
    xn_ref[...] = xn.astype(xn_ref.dtype)
    logits = jnp.dot(xn, wr_ref[...], preferred_element_type=F32,
                     precision=lax.Precision.HIGHEST) + br_ref[...]
    lane = lax.broadcasted_iota(jnp.int32, logits.shape, 1)
    vals = []
    for k in range(TOP_K):
        m = jnp.max(logits, axis=-1, keepdims=True)
        idx = jnp.min(jnp.where(logits == m, lane, N_EXPERTS), axis=-1, keepdims=True)
        e_ref[:, k:k + 1] = idx
        vals.append(m)
        logits = jnp.where(lane == idx, -jnp.inf, logits)
    ex = [jnp.exp(v - vals[0]) for v in vals]
    den = ex[0]
    for k in range(1, TOP_K):
        den = den + ex[k]
    for k in range(TOP_K):
        gate_ref[:, k:k + 1] = ex[k] / den


def _router(h, g, w_router, b_router, tm):
    n = h.shape[0]
    return pl.pallas_call(
        _router_kernel,
        grid=(n // tm,),
        in_specs=[pl.BlockSpec((tm, D_MODEL), lambda i: (i, 0)),
                  pl.BlockSpec((1, D_MODEL), lambda i: (0, 0)),
                  pl.BlockSpec((D_MODEL, N_EXPERTS), lambda i: (0, 0)),
                  pl.BlockSpec((1, N_EXPERTS), lambda i: (0, 0))],
        out_specs=[pl.BlockSpec((tm, D_MODEL), lambda i: (i, 0)),
                   pl.BlockSpec((tm, TOP_K), lambda i: (i, 0)),
                   pl.BlockSpec((tm, TOP_K), lambda i: (i, 0))],
        out_shape=[jax.ShapeDtypeStruct((n, D_MODEL), BF16),
                   jax.ShapeDtypeStruct((n, TOP_K), jnp.int32),
                   jax.ShapeDtypeStruct((n, TOP_K), F32)],
        compiler_params=_params(("parallel",)),
        name="router",
    )(h, g.reshape(1, D_MODEL), w_router, b_router.reshape(1, N_EXPERTS))


MOE_TM = 576
MOE_TB = 1024
MOE_KQ = D_MODEL // MOE_TB
MOE_NG = D_FF // MOE_TB
MOE_NO = D_MODEL // MOE_TB
MOE_P1 = MOE_NG * MOE_KQ
MOE_P2 = MOE_NO * MOE_KQ


def _experts_kernel(te_ref, tv_ref, *refs):
    x = refs[0:MOE_KQ]
    (wg_ref, wu_ref, bg_ref, bu_ref, wd_ref, bd_ref, gate_ref, o_ref,
     g_ref, u_ref, h_ref) = refs[MOE_KQ:]
    i = pl.program_id(0)
    s = pl.program_id(1)
    valid = tv_ref[i] > 0
    first = s < MOE_P1
    kq = s % MOE_KQ

    for q in range(MOE_KQ):
        @pl.when(jnp.logical_and(jnp.logical_and(valid, first), kq == q))
        def _():
            xq = x[q][...]
            pg = jnp.dot(xq, wg_ref[...].astype(BF16), preferred_element_type=F32)
            pu = jnp.dot(xq, wu_ref[...].astype(BF16), preferred_element_type=F32)
            if q == 0:
                g_ref[...] = pg
                u_ref[...] = pu
            else:
                g_ref[...] += pg
                u_ref[...] += pu

    @pl.when(jnp.logical_and(jnp.logical_and(valid, first), kq == MOE_KQ - 1))
    def _():
        glu = jnp.minimum(g_ref[...] + bg_ref[...], SWIGLU_LIMIT)
        lin = jnp.clip(u_ref[...] + bu_ref[...], -SWIGLU_LIMIT, SWIGLU_LIMIT)
        h = glu * jax.nn.sigmoid(SWIGLU_ALPHA * glu) * (lin + 1.0)
        h_ref[s // MOE_KQ] = h.astype(BF16)

    second = jnp.logical_and(valid, jnp.logical_not(first))
    for q in range(MOE_KQ):
        @pl.when(jnp.logical_and(second, kq == q))
        def _():
            y = jnp.dot(h_ref[q], wd_ref[...].astype(BF16), preferred_element_type=F32)
            if q == 0:
                o_ref[...] = y + bd_ref[...]
            else:
                o_ref[...] += y

    @pl.when(jnp.logical_and(second, kq == MOE_KQ - 1))
    def _():
        o_ref[...] = o_ref[...] * gate_ref[...]

    @pl.when(jnp.logical_and(jnp.logical_not(valid), jnp.logical_not(first)))
    def _():
        o_ref[...] = jnp.zeros_like(o_ref)


def _experts(xs, slot_gate, tile_expert, tile_valid, w_gate_up, b_gate_up, w_down, b_down):
    n_slots = xs.shape[0]
    n_tiles = n_slots // MOE_TM

    def p1(s, tv, i):
        s1 = jnp.where(tv[i] > 0, jnp.minimum(s, MOE_P1 - 1), MOE_P1 - 1)
        return s1 % MOE_KQ, s1 // MOE_KQ

    def p2(s):
        s2 = jnp.clip(s - MOE_P1, 0, MOE_P2 - 1)
        return s2 % MOE_KQ, s2 // MOE_KQ

    def p2_w(s, tv, i):
        s2 = jnp.where(tv[i] > 0, jnp.clip(s - MOE_P1, 0, MOE_P2 - 1), MOE_P2 - 1)
        return s2 % MOE_KQ, s2 // MOE_KQ

    def gate_up_spec(col0):
        def index(i, s, te, tv):
            piece, group = p1(s, tv, i)
            return te[i], piece, col0 + group
        return pl.BlockSpec((None, MOE_TB, MOE_TB), index)

    def gate_up_bias_spec(col0):
        return pl.BlockSpec((None, 1, MOE_TB),
                            lambda i, s, te, tv: (te[i], 0, col0 + p1(s, tv, i)[1]))

    def down_index(i, s, te, tv):
        piece, group = p2_w(s, tv, i)
        return te[i], piece, group

    grid_spec = pltpu.PrefetchScalarGridSpec(
        num_scalar_prefetch=2,
        grid=(jnp.sum(tile_valid), MOE_P1 + MOE_P2),
        in_specs=(
            [pl.BlockSpec((MOE_TM, MOE_TB), lambda i, s, te, tv, q=q: (i, q),
                          pipeline_mode=pl.Buffered(1)) for q in range(MOE_KQ)]
            + [gate_up_spec(0), gate_up_spec(MOE_NG),
               gate_up_bias_spec(0), gate_up_bias_spec(MOE_NG),
               pl.BlockSpec((None, MOE_TB, MOE_TB), down_index),
               pl.BlockSpec((None, 1, MOE_TB), lambda i, s, te, tv: (te[i], 0, p2_w(s, tv, i)[1])),
               pl.BlockSpec((MOE_TM, 1), lambda i, s, te, tv: (i, 0))]
        ),
        out_specs=pl.BlockSpec((MOE_TM, MOE_TB), lambda i, s, te, tv: (i, p2(s)[1])),
        scratch_shapes=[pltpu.VMEM((MOE_TM, MOE_TB), F32),
                        pltpu.VMEM((MOE_TM, MOE_TB), F32),
                        pltpu.VMEM((MOE_NG, MOE_TM, MOE_TB), BF16)],
    )
    b_gu = b_gate_up.reshape(N_EXPERTS, 1, 2 * D_FF)
    return pl.pallas_call(
        _experts_kernel,
        grid_spec=grid_spec,
        out_shape=jax.ShapeDtypeStruct((n_slots, D_MODEL), F32),
        compiler_params=_params(("arbitrary", "arbitrary")),
        name="experts",
    )(tile_expert, tile_valid, *([xs] * MOE_KQ), w_gate_up, w_gate_up, b_gu, b_gu,
      w_down, b_down.reshape(N_EXPERTS, 1, D_MODEL), slot_gate)


def _dispatch(top_e, gate):
    n = top_e.shape[0]
    n_assign = n * TOP_K
    n_tiles = -(-n_assign // MOE_TM) + N_EXPERTS
    n_slots = n_tiles * MOE_TM
    flat_e = top_e.reshape(-1)
    onehot = (flat_e[:, None] == jnp.arange(N_EXPERTS, dtype=jnp.int32)[None, :]).astype(jnp.int32)
    csum = jnp.cumsum(onehot, axis=0)
    rank = jnp.sum(csum * onehot, axis=1) - 1
    counts = csum[-1]
    tiles = (counts + MOE_TM - 1) // MOE_TM
    tile_end = jnp.cumsum(tiles)
    pad_start = (tile_end - tiles) * MOE_TM
    dest = pad_start[flat_e] + rank
    tok = jnp.arange(n_assign, dtype=jnp.int32) // TOP_K
    gate_bits = lax.bitcast_convert_type(gate.reshape(-1), jnp.int32)
    pad_tok = jnp.arange(n_slots, dtype=jnp.int32) % n
    slots = jnp.stack([pad_tok, jnp.zeros((n_slots,), jnp.int32)], axis=1)
    slots = slots.at[dest].set(jnp.stack([tok, gate_bits], axis=1))
    slot_tok = slots[:, 0]
    slot_gate = lax.bitcast_convert_type(slots[:, 1], F32)
    tile_ids = jnp.arange(n_tiles, dtype=jnp.int32)
    tile_valid = (tile_ids < tile_end[-1]).astype(jnp.int32)
    last = jnp.maximum(tile_end[-1] - 1, 0)
    tile_expert = jnp.sum(
        (tile_end[None, :] <= jnp.minimum(tile_ids, last)[:, None]).astype(jnp.int32), axis=1)
    tile_expert = jnp.minimum(tile_expert, N_EXPERTS - 1).astype(jnp.int32)
    return slot_tok, slot_gate.reshape(n_slots, 1), tile_expert, tile_valid, dest.reshape(n, TOP_K)


def _final_kernel(h_ref, *refs):
    y_refs, g_ref, o_ref = refs[:TOP_K], refs[TOP_K], refs[TOP_K + 1]
    h = h_ref[...]
    for y_ref in y_refs:
        h = h + y_ref[...]
    o_ref[...] = _rms(h) * g_ref[...]


def _final(h, y_rows, g, row0, n_rows, tm):
    n = h.shape[0]
    blk0 = row0 // tm
    y_specs = [pl.BlockSpec((tm, D_MODEL), lambda i, k=k: (k * (n // tm) + blk0 + i, 0))
               for k in range(TOP_K)]
    return pl.pallas_call(
        _final_kernel,
        grid=(n_rows // tm,),
        in_specs=[pl.BlockSpec((tm, D_MODEL), lambda i: (blk0 + i, 0))] + y_specs
                 + [pl.BlockSpec((1, D_MODEL), lambda i: (0, 0))],
        out_specs=pl.BlockSpec((tm, D_MODEL), lambda i: (i, 0)),
        out_shape=jax.ShapeDtypeStruct((n_rows, D_MODEL), F32),
        compiler_params=_params(("parallel",)),
        name="final_norm",
    )(h, *([y_rows] * TOP_K), g.reshape(1, D_MODEL))


def kernel(x_prompt, x_sample, cache_k, cache_v, state_conv, norm_mix, w_in, conv_w,
           norm_attn_out, norm_conv_out, w_out, norm_ffn, w_router, b_router,
           w_gate_up, b_gate_up, w_down, b_down, norm_final):
    batch, seq, _ = x_prompt.shape
    bsz, t_new, _ = x_sample.shape
    depth = w_in.shape[0]
    assert depth == 1 and t_new >= CONV_WIDTH - 1
    n_p, n_s = batch * seq, bsz * t_new
    n = n_p + n_s
    layer = 0

    x_all = jnp.concatenate([x_prompt.reshape(n_p, D_MODEL), x_sample.reshape(n_s, D_MODEL)], axis=0)
    xn = _norm_cast(x_all, norm_mix[layer], 512)
    proj = _matmul(xn, w_in[layer].astype(BF16), None, 512, 1024)
    proj_s = lax.slice(proj, (n_p, 0), (n, D_IN))
    proj_s4 = proj_s.reshape(bsz, t_new, D_IN // HEAD_DIM, HEAD_DIM)

    attn_p = _prompt_attention(proj, batch, seq)
    attn_s = _sample_attention(proj_s4, cache_k[layer], cache_v[layer])
    conv_p, u_last_p = _prompt_conv(proj, conv_w[layer], batch, seq)
    conv_s, u_s = _sample_conv(proj_s[:, 3 * D_ATT:].reshape(bsz, t_new * 3 * D_CONV),
                               state_conv[layer].reshape(bsz, (CONV_WIDTH - 1) * D_CONV),
                               conv_w[layer])

    attn = jnp.concatenate([attn_p, attn_s.reshape(n_s, D_ATT)], axis=0)
    conv = jnp.concatenate([conv_p, conv_s.reshape(n_s, D_CONV)], axis=0)
    cat = _mixer_norm(attn, conv, norm_attn_out[layer], norm_conv_out[layer], 512)
    h1 = _matmul(cat, w_out[layer].astype(BF16), x_all, 512, 1024)

    xn2, top_e, gate = _router(h1, norm_ffn[layer], w_router[layer], b_router[layer], 512)
    slot_tok, slot_gate, tile_expert, tile_valid, dest = _dispatch(top_e, gate)
    xs = xn2[slot_tok]
    ys = _experts(xs, slot_gate, tile_expert, tile_valid, w_gate_up[layer], b_gate_up[layer],
                  w_down[layer], b_down[layer])
    y_rows = ys[dest.T.reshape(-1)]
    y_prompt = _final(h1, y_rows, norm_final, 0, n_p, 128).reshape(batch, seq, D_MODEL)
    y_sample = _final(h1, y_rows, norm_final, n_p, n_s, 128).reshape(bsz, t_new, D_MODEL)

    n_keep = min(MAX_WINDOW, seq)

    def prompt_window(col0):
        parts = [lax.slice(proj, ((b + 1) * seq - n_keep, col0), ((b + 1) * seq, col0 + D_ATT))
                 for b in range(batch)]
        return jnp.stack(parts).reshape(1, batch, n_keep, N_HEADS, HEAD_DIM)

    k_prompt_win = prompt_window(D_ATT)
    v_prompt_win = prompt_window(2 * D_ATT)
    conv_prompt = u_last_p[None]
    k_sample_new = proj_s4[:, :, N_HEADS:2 * N_HEADS][None]
    v_sample_new = proj_s4[:, :, 2 * N_HEADS:3 * N_HEADS][None]
    conv_sample = u_s.reshape(bsz, t_new, D_CONV)[:, t_new - (CONV_WIDTH - 1):][None]
    return (y_prompt, y_sample, k_prompt_win, v_prompt_win, conv_prompt,
            k_sample_new, v_sample_new, conv_sample)
```

```python
import functools

import numpy as np
import jax
import jax.numpy as jnp
from jax import lax
from jax.experimental import pallas as pl
from jax.experimental.pallas import tpu as pltpu

D_MODEL = 4096
HEAD_DIM = 128
N_HEADS = 24
D_ATT = N_HEADS * HEAD_DIM
D_CONV = D_MODEL - D_ATT
CONV_WIDTH = 3
BRANCHES = ((128, 1), (512, 4), (2048, 16))
MAX_WINDOW = 2048
D_IN = 3 * D_ATT + 3 * D_CONV
N_EXPERTS = 32
TOP_K = 4
D_FF = D_MODEL
SWIGLU_ALPHA = 1.702
SWIGLU_LIMIT = 7.0
RMS_EPS = 1e-5

F32 = jnp.float32
BF16 = jnp.bfloat16

VMEM_LIMIT_BYTES = 56 * 1024 * 1024


def _params(semantics):
    return pltpu.CompilerParams(dimension_semantics=semantics,
                                vmem_limit_bytes=VMEM_LIMIT_BYTES)


def _branch_weight(dist):
    dist = np.asarray(dist)
    c = np.zeros(dist.shape, np.float32)
    for window, dil in BRANCHES:
        c += ((dist >= 0) & (dist <= window) & (dist % dil == 0)).astype(np.float32)
    return c


def _rms(x):
    return x * lax.rsqrt(jnp.mean(x * x, axis=-1, keepdims=True) + RMS_EPS)


def _norm_cast_kernel(x_ref, g_ref, o_ref):
    o_ref[...] = (_rms(x_ref[...]) * g_ref[...]).astype(o_ref.dtype)


def _norm_cast(x, g, tm):
    n, d = x.shape
    return pl.pallas_call(
        _norm_cast_kernel,
        grid=(n // tm,),
        in_specs=[pl.BlockSpec((tm, d), lambda i: (i, 0)),
                  pl.BlockSpec((1, d), lambda i: (0, 0))],
        out_specs=pl.BlockSpec((tm, d), lambda i: (i, 0)),
        out_shape=jax.ShapeDtypeStruct((n, d), BF16),
        compiler_params=_params(("parallel",)),
        name="norm_cast",
    )(x, g.reshape(1, d))


def _mm_kernel(x_ref, w_ref, o_ref):
    o_ref[...] = jnp.dot(x_ref[...], w_ref[...], preferred_element_type=F32)


def _mm_res_kernel(x_ref, w_ref, r_ref, o_ref):
    o_ref[...] = r_ref[...] + jnp.dot(x_ref[...], w_ref[...], preferred_element_type=F32)


def _matmul(x, w, residual, tm, tn):
    n, k = x.shape
    m = w.shape[1]
    in_specs = [pl.BlockSpec((tm, k), lambda j, i: (i, 0)),
                pl.BlockSpec((k, tn), lambda j, i: (0, j))]
    args = [x, w]
    body = _mm_kernel
    if residual is not None:
        in_specs.append(pl.BlockSpec((tm, tn), lambda j, i: (i, j)))
        args.append(residual)
        body = _mm_res_kernel
    return pl.pallas_call(
        body,
        grid=(m // tn, n // tm),
        in_specs=in_specs,
        out_specs=pl.BlockSpec((tm, tn), lambda j, i: (i, j)),
        out_shape=jax.ShapeDtypeStruct((n, m), F32),
        compiler_params=_params(("parallel", "parallel")),
        name="proj_res" if residual is not None else "proj",
    )(*args)


ATT_TQ = 256
ATT_HEADS = 4
ATT_CHUNKS = MAX_WINDOW // ATT_TQ + 1


def _prompt_attn_kernel(q_ref, k_ref, v_ref, c_ref, o_ref):
    qb = pl.program_id(2)
    lanes = [slice(h * HEAD_DIM, (h + 1) * HEAD_DIM) for h in range(ATT_HEADS)]
    qs = [(q_ref[:, ln] * (HEAD_DIM ** -0.5)).astype(BF16) for ln in lanes]

    def chunk(c, carry):
        start = pl.multiple_of((qb - c) * ATT_TQ, ATT_TQ)
        cw = c_ref[c]
        out = []
        for q, ln, (m, l, acc) in zip(qs, lanes, carry):
            kc = k_ref[pl.ds(start, ATT_TQ), ln].astype(BF16)
            vc = v_ref[pl.ds(start, ATT_TQ), ln].astype(BF16)
            s = lax.dot_general(q, kc, (((1,), (1,)), ((), ())), preferred_element_type=F32)
            s = jnp.where(cw > 0, s, -jnp.inf)
            m_new = jnp.maximum(m, jnp.max(s, axis=-1, keepdims=True))
            p = cw * jnp.exp(s - m_new)
            alpha = jnp.exp(m - m_new)
            l = alpha * l + jnp.sum(p, axis=-1, keepdims=True)
            acc = alpha * acc + jnp.dot(p.astype(BF16), vc, preferred_element_type=F32)
            out.append((m_new, l, acc))
        return tuple(out)

    init = tuple((jnp.full((ATT_TQ, 1), -jnp.inf, F32), jnp.zeros((ATT_TQ, 1), F32),
                  jnp.zeros((ATT_TQ, HEAD_DIM), F32)) for _ in range(ATT_HEADS))
    n_chunks = jnp.minimum(qb, ATT_CHUNKS - 1) + 1
    final = lax.fori_loop(0, n_chunks, chunk, init)
    for ln, (_, l, acc) in zip(lanes, final):
        o_ref[:, ln] = acc / l


def _prompt_attention(proj, batch, seq):
    qi = np.arange(ATT_TQ)[:, None]
    kj = np.arange(ATT_TQ)[None, :]
    table = np.stack([_branch_weight(qi - kj + ATT_TQ * c) for c in range(ATT_CHUNKS)])
    nqb = seq // ATT_TQ
    groups = N_HEADS // ATT_HEADS
    width = ATT_HEADS * HEAD_DIM
    return pl.pallas_call(
        _prompt_attn_kernel,
        grid=(batch, groups, nqb),
        in_specs=[
            pl.BlockSpec((ATT_TQ, width), lambda b, h, i: (b * nqb + i, h)),
            pl.BlockSpec((seq, width), lambda b, h, i: (b, groups + h)),
            pl.BlockSpec((seq, width), lambda b, h, i: (b, 2 * groups + h)),
            pl.BlockSpec((ATT_CHUNKS, ATT_TQ, ATT_TQ), lambda b, h, i: (0, 0, 0)),
        ],
        out_specs=pl.BlockSpec((ATT_TQ, width), lambda b, h, i: (b * nqb + i, h)),
        out_shape=jax.ShapeDtypeStruct((batch * seq, D_ATT), F32),
        compiler_params=_params(("parallel", "parallel", "parallel")),
        name="prompt_attn",
    )(proj, proj, proj, jnp.asarray(table))


SA_HEADS = 8
SA_NEAR = 512
LANE = 128


def _sample_attn_kernel(q_ref, kn_ref, vn_ref, *refs):
    t_new = q_ref.shape[0]
    kfar = refs[0:t_new]
    knear = refs[t_new]
    vfar = refs[t_new + 1:2 * t_new + 1]
    vnear = refs[2 * t_new + 1]
    c_ref, o_ref, kb_ref, vb_ref = refs[2 * t_new + 2:]
    n_keys = kb_ref.shape[0]

    def put(dst, off, src):
        rows = src.shape[0] * SA_HEADS
        dst[off:off + rows, :] = src[...].reshape(rows, HEAD_DIM).astype(BF16)
        return off + rows

    off = 0
    for k_src, v_src in list(zip(kfar, vfar)) + [(knear, vnear), (kn_ref, vn_ref)]:
        put(kb_ref, off, k_src)
        off = put(vb_ref, off, v_src)
    kb_ref[off:, :] = jnp.zeros((n_keys - off, HEAD_DIM), BF16)
    vb_ref[off:, :] = jnp.zeros((n_keys - off, HEAD_DIM), BF16)

    rows = t_new * SA_HEADS
    q = (q_ref[...].reshape(rows, HEAD_DIM) * (HEAD_DIM ** -0.5)).astype(BF16)
    s = lax.dot_general(q, kb_ref[...], (((1,), (1,)), ((), ())), preferred_element_type=F32)
    cw = c_ref[...]
    s = jnp.where(cw > 0, s, -jnp.inf)
    m = jnp.max(s, axis=-1, keepdims=True)
    p = cw * jnp.exp(s - m)
    l = jnp.sum(p, axis=-1, keepdims=True)
    o = jnp.dot(p.astype(BF16), vb_ref[...], preferred_element_type=F32) / l
    o_ref[...] = o.reshape(t_new, SA_HEADS, HEAD_DIM)


def _sample_attention(proj_s, cache_k, cache_v):
    bsz, t_new = proj_s.shape[:2]
    n_past = cache_k.shape[1]
    dil = BRANCHES[-1][1]
    assert n_past == MAX_WINDOW and n_past % dil == 0 and t_new <= dil
    assert all(w <= SA_NEAR for w, _ in BRANCHES[:-1])
    n_far = (n_past - SA_NEAR) // dil
    n_groups = n_past // dil
    n_keys = -(-((t_new * n_far + SA_NEAR + t_new) * SA_HEADS) // LANE) * LANE

    pos_list = [dil * np.arange(n_far) + (n_past + r) % dil for r in range(t_new)]
    pos_list += [n_past - SA_NEAR + np.arange(SA_NEAR), n_past + np.arange(t_new)]
    pos_used = np.repeat(np.concatenate(pos_list), SA_HEADS)
    pos = np.full((n_keys,), -1, np.int64)
    pos[:pos_used.size] = pos_used
    head = np.arange(n_keys) % SA_HEADS
    row = np.arange(t_new * SA_HEADS)
    dist = (n_past + row // SA_HEADS)[:, None] - pos[None, :]
    keep = (pos[None, :] >= 0) & (head[None, :] == (row % SA_HEADS)[:, None])
    table = np.where(keep, _branch_weight(dist), 0.0).astype(np.float32)

    def new_spec(which):
        return pl.BlockSpec((None, t_new, SA_HEADS, HEAD_DIM),
                            lambda b, g, which=which: (b, 0, which * (N_HEADS // SA_HEADS) + g, 0))

    def far_spec(r):
        res = (n_past + r) % dil
        return pl.BlockSpec((None, n_far, None, SA_HEADS, HEAD_DIM),
                            lambda b, g, res=res: (b, 0, res, g, 0))

    near_spec = pl.BlockSpec((None, SA_NEAR, SA_HEADS, HEAD_DIM),
                             lambda b, g: (b, n_past // SA_NEAR - 1, g, 0))
    ck_far = cache_k.reshape(bsz, n_groups, dil, N_HEADS, HEAD_DIM)
    cv_far = cache_v.reshape(bsz, n_groups, dil, N_HEADS, HEAD_DIM)
    in_specs = ([new_spec(0), new_spec(1), new_spec(2)]
                + [far_spec(r) for r in range(t_new)] + [near_spec]
                + [far_spec(r) for r in range(t_new)] + [near_spec]
                + [pl.BlockSpec((t_new * SA_HEADS, n_keys), lambda b, g: (0, 0))])
    args = ([proj_s] * 3 + [ck_far] * t_new + [cache_k] + [cv_far] * t_new + [cache_v]
            + [jnp.asarray(table)])
    return pl.pallas_call(
        _sample_attn_kernel,
        grid=(bsz, N_HEADS // SA_HEADS),
        in_specs=in_specs,
        out_specs=pl.BlockSpec((None, t_new, SA_HEADS, HEAD_DIM), lambda b, g: (b, 0, g, 0)),
        out_shape=jax.ShapeDtypeStruct((bsz, t_new, N_HEADS, HEAD_DIM), F32),
        scratch_shapes=[pltpu.VMEM((n_keys, HEAD_DIM), BF16),
                        pltpu.VMEM((n_keys, HEAD_DIM), BF16)],
        compiler_params=_params(("parallel", "parallel")),
        name="sample_attn",
    )(*args)


CONV_TC = 128
CONV_HALO = 8


def _prompt_conv_kernel(b_ref, c_ref, x_ref, w_ref, o_ref, last_ref, u_ref):
    seq = b_ref.shape[0]
    u = c_ref[...] * x_ref[...]
    u_ref[0:CONV_HALO, :] = jnp.zeros((CONV_HALO, CONV_TC), F32)
    u_ref[CONV_HALO:, :] = u
    w = w_ref[...]
    acc = w[CONV_WIDTH - 1:CONV_WIDTH, :] * u
    for tap in range(CONV_WIDTH - 1):
        shift = CONV_WIDTH - 1 - tap
        acc = acc + w[tap:tap + 1, :] * u_ref[CONV_HALO - shift:CONV_HALO - shift + seq, :]
    o_ref[...] = b_ref[...] * acc
    last_ref[0] = u_ref[CONV_HALO + seq - (CONV_WIDTH - 1):, :]


def _prompt_conv(proj, conv_w, batch, seq):
    base = 3 * D_ATT // CONV_TC
    per = D_CONV // CONV_TC
    return pl.pallas_call(
        _prompt_conv_kernel,
        grid=(batch, per),
        in_specs=[pl.BlockSpec((seq, CONV_TC), lambda b, j: (b, base + j)),
                  pl.BlockSpec((seq, CONV_TC), lambda b, j: (b, base + per + j)),
                  pl.BlockSpec((seq, CONV_TC), lambda b, j: (b, base + 2 * per + j)),
                  pl.BlockSpec((CONV_WIDTH, CONV_TC), lambda b, j: (0, j))],
        out_specs=[pl.BlockSpec((seq, CONV_TC), lambda b, j: (b, j)),
                   pl.BlockSpec((1, CONV_WIDTH - 1, CONV_TC), lambda b, j: (b, 0, j))],
        out_shape=[jax.ShapeDtypeStruct((batch * seq, D_CONV), F32),
                   jax.ShapeDtypeStruct((batch, CONV_WIDTH - 1, D_CONV), F32)],
        scratch_shapes=[pltpu.VMEM((CONV_HALO + seq, CONV_TC), F32)],
        compiler_params=_params(("parallel", "parallel")),
        name="prompt_conv",
    )(proj, proj, proj, conv_w)


def _sample_conv_kernel(t_new, g_ref, st_ref, w_ref, o_ref, u_ref):
    w = w_ref[...]
    ext = [st_ref[:, i * D_CONV:(i + 1) * D_CONV] for i in range(CONV_WIDTH - 1)]
    for t in range(t_new):
        base = t * 3 * D_CONV
        u = g_ref[:, base + D_CONV:base + 2 * D_CONV] * g_ref[:, base + 2 * D_CONV:base + 3 * D_CONV]
        u_ref[:, t * D_CONV:(t + 1) * D_CONV] = u
        ext.append(u)
    for t in range(t_new):
        acc = w[0:1, :] * ext[t]
        for tap in range(1, CONV_WIDTH):
            acc = acc + w[tap:tap + 1, :] * ext[t + tap]
        o_ref[:, t * D_CONV:(t + 1) * D_CONV] = g_ref[:, t * 3 * D_CONV:t * 3 * D_CONV + D_CONV] * acc


def _sample_conv(gates, state, conv_w):
    bsz = gates.shape[0]
    t_new = gates.shape[1] // (3 * D_CONV)
    full = lambda a: pl.BlockSpec(a.shape, lambda i: (0,) * a.ndim)
    out = jax.ShapeDtypeStruct((bsz, t_new * D_CONV), F32)
    return pl.pallas_call(
        functools.partial(_sample_conv_kernel, t_new),
        grid=(1,),
        in_specs=[full(gates), full(state), full(conv_w)],
        out_specs=[pl.BlockSpec(out.shape, lambda i: (0, 0))] * 2,
        out_shape=[out, out],
        compiler_params=_params(("arbitrary",)),
        name="sample_conv",
    )(gates, state, conv_w)


def _mixer_norm_kernel(a_ref, c_ref, ga_ref, gc_ref, o_ref):
    o_ref[:, :D_ATT] = (_rms(a_ref[...]) * ga_ref[...]).astype(o_ref.dtype)
    o_ref[:, D_ATT:] = (_rms(c_ref[...]) * gc_ref[...]).astype(o_ref.dtype)


def _mixer_norm(attn, conv, g_attn, g_conv, tm):
    n = attn.shape[0]
    return pl.pallas_call(
        _mixer_norm_kernel,
        grid=(n // tm,),
        in_specs=[pl.BlockSpec((tm, D_ATT), lambda i: (i, 0)),
                  pl.BlockSpec((tm, D_CONV), lambda i: (i, 0)),
                  pl.BlockSpec((1, D_ATT), lambda i: (0, 0)),
                  pl.BlockSpec((1, D_CONV), lambda i: (0, 0))],
        out_specs=pl.BlockSpec((tm, D_MODEL), lambda i: (i, 0)),
        out_shape=jax.ShapeDtypeStruct((n, D_MODEL), BF16),
        compiler_params=_params(("parallel",)),
        name="mixer_norm",
    )(attn, conv, g_attn.reshape(1, D_ATT), g_conv.reshape(1, D_CONV))


def _router_kernel(h_ref, g_ref, wr_ref, br_ref, xn_ref, e_ref, gate_ref):
    xn = _rms(h_ref[...]) * g_ref[...]
    xn_ref[...] = xn.astype(xn_ref.dtype)
    logits = jnp.dot(xn, wr_ref[...], preferred_element_type=F32,
                     precision=lax.Precision.HIGHEST) + br_ref[...]
    lane = lax.broadcasted_iota(jnp.int32, logits.shape, 1)
    vals = []
    for k in range(TOP_K):
        m = jnp.max(logits, axis=-1, keepdims=True)
        idx = jnp.min(jnp.where(logits == m, lane, N_EXPERTS), axis=-1, keepdims=True)
        e_ref[:, k:k + 1] = idx
        vals.append(m)
        logits = jnp.where(lane == idx, -jnp.inf, logits)
    ex = [jnp.exp(v - vals[0]) for v in vals]
    den = ex[0]
    for k in range(1, TOP_K):
        den = den + ex[k]
    for k in range(TOP_K):
        gate_ref[:, k:k + 1] = ex[k] / den


def _router(h, g, w_router, b_router, tm):
    n = h.shape[0]
    return pl.pallas_call(
        _router_kernel,
        grid=(n // tm,),
        in_specs=[pl.BlockSpec((tm, D_MODEL), lambda i: (i, 0)),
                  pl.BlockSpec((1, D_MODEL), lambda i: (0, 0)),
                  pl.BlockSpec((D_MODEL, N_EXPERTS), lambda i: (0, 0)),
                  pl.BlockSpec((1, N_EXPERTS), lambda i: (0, 0))],
        out_specs=[pl.BlockSpec((tm, D_MODEL), lambda i: (i, 0)),
                   pl.BlockSpec((tm, TOP_K), lambda i: (i, 0)),
                   pl.BlockSpec((tm, TOP_K), lambda i: (i, 0))],
        out_shape=[jax.ShapeDtypeStruct((n, D_MODEL), BF16),
                   jax.ShapeDtypeStruct((n, TOP_K), jnp.int32),
                   jax.ShapeDtypeStruct((n, TOP_K), F32)],
        compiler_params=_params(("parallel",)),
        name="router",
    )(h, g.reshape(1, D_MODEL), w_router, b_router.reshape(1, N_EXPERTS))


MOE_TM = 1152
MOE_TB = 1024
MOE_TC = 512
MOE_KQ = D_MODEL // MOE_TB
MOE_NG = D_FF // MOE_TC
MOE_NO = D_MODEL // MOE_TC
MOE_P1 = MOE_NG * MOE_KQ
MOE_P2 = MOE_NO * MOE_KQ


def _experts_kernel(te_ref, tv_ref, *refs):
    x = refs[0:MOE_KQ]
    (wg_ref, wu_ref, bg_ref, bu_ref, wd_ref, bd_ref, gate_ref, o_ref,
     g_ref, u_ref, h_ref) = refs[MOE_KQ:]
    i = pl.program_id(0)
    s = pl.program_id(1)
    valid = tv_ref[i] > 0
    first = s < MOE_P1
    kq = s % MOE_KQ

    for q in range(MOE_KQ):
        @pl.when(jnp.logical_and(jnp.logical_and(valid, first), kq == q))
        def _():
            xq = x[q][...]
            pg = jnp.dot(xq, wg_ref[...].astype(BF16), preferred_element_type=F32)
            pu = jnp.dot(xq, wu_ref[...].astype(BF16), preferred_element_type=F32)
            if q == 0:
                g_ref[...] = pg
                u_ref[...] = pu
            else:
                g_ref[...] += pg
                u_ref[...] += pu

    @pl.when(jnp.logical_and(jnp.logical_and(valid, first), kq == MOE_KQ - 1))
    def _():
        glu = jnp.minimum(g_ref[...] + bg_ref[...], SWIGLU_LIMIT)
        lin = jnp.clip(u_ref[...] + bu_ref[...], -SWIGLU_LIMIT, SWIGLU_LIMIT)
        h = glu * jax.nn.sigmoid(SWIGLU_ALPHA * glu) * (lin + 1.0)
        h_ref[s // MOE_KQ] = h.astype(BF16)

    second = jnp.logical_and(valid, jnp.logical_not(first))
    for q in range(MOE_KQ):
        @pl.when(jnp.logical_and(second, kq == q))
        def _():
            y = (jnp.dot(h_ref[2 * q], wd_ref[:MOE_TC, :].astype(BF16), preferred_element_type=F32)
                 + jnp.dot(h_ref[2 * q + 1], wd_ref[MOE_TC:, :].astype(BF16),
                           preferred_element_type=F32))
            if q == 0:
                o_ref[...] = y + bd_ref[...]
            else:
                o_ref[...] += y

    @pl.when(jnp.logical_and(second, kq == MOE_KQ - 1))
    def _():
        o_ref[...] = o_ref[...] * gate_ref[...]

    @pl.when(jnp.logical_and(jnp.logical_not(valid), jnp.logical_not(first)))
    def _():
        o_ref[...] = jnp.zeros_like(o_ref)


def _experts(xs, slot_gate, tile_expert, tile_valid, w_gate_up, b_gate_up, w_down, b_down):
    n_slots = xs.shape[0]
    n_tiles = n_slots // MOE_TM

    def p1(s, tv, i):
        s1 = jnp.where(tv[i] > 0, jnp.minimum(s, MOE_P1 - 1), MOE_P1 - 1)
        return s1 % MOE_KQ, s1 // MOE_KQ

    def p2(s):
        s2 = jnp.clip(s - MOE_P1, 0, MOE_P2 - 1)
        return s2 % MOE_KQ, s2 // MOE_KQ

    def p2_w(s, tv, i):
        s2 = jnp.where(tv[i] > 0, jnp.clip(s - MOE_P1, 0, MOE_P2 - 1), MOE_P2 - 1)
        return s2 % MOE_KQ, s2 // MOE_KQ

    def gate_up_spec(col0):
        def index(i, s, te, tv):
            piece, group = p1(s, tv, i)
            return te[i], piece, col0 + group
        return pl.BlockSpec((None, MOE_TB, MOE_TC), index)

    def gate_up_bias_spec(col0):
        return pl.BlockSpec((None, 1, MOE_TC),
                            lambda i, s, te, tv: (te[i], 0, col0 + p1(s, tv, i)[1]))

    def down_index(i, s, te, tv):
        piece, group = p2_w(s, tv, i)
        return te[i], piece, group

    grid_spec = pltpu.PrefetchScalarGridSpec(
        num_scalar_prefetch=2,
        grid=(jnp.sum(tile_valid), MOE_P1 + MOE_P2),
        in_specs=(
            [pl.BlockSpec((MOE_TM, MOE_TB), lambda i, s, te, tv, q=q: (i, q),
                          pipeline_mode=pl.Buffered(1)) for q in range(MOE_KQ)]
            + [gate_up_spec(0), gate_up_spec(MOE_NG),
               gate_up_bias_spec(0), gate_up_bias_spec(MOE_NG),
               pl.BlockSpec((None, MOE_TB, MOE_TC), down_index),
               pl.BlockSpec((None, 1, MOE_TC), lambda i, s, te, tv: (te[i], 0, p2_w(s, tv, i)[1])),
               pl.BlockSpec((MOE_TM, 1), lambda i, s, te, tv: (i, 0))]
        ),
        out_specs=pl.BlockSpec((MOE_TM, MOE_TC), lambda i, s, te, tv: (i, p2(s)[1])),
        scratch_shapes=[pltpu.VMEM((MOE_TM, MOE_TC), F32),
                        pltpu.VMEM((MOE_TM, MOE_TC), F32),
                        pltpu.VMEM((MOE_NG, MOE_TM, MOE_TC), BF16)],
    )
    b_gu = b_gate_up.reshape(N_EXPERTS, 1, 2 * D_FF)
    return pl.pallas_call(
        _experts_kernel,
        grid_spec=grid_spec,
        out_shape=jax.ShapeDtypeStruct((n_slots, D_MODEL), F32),
        compiler_params=_params(("arbitrary", "arbitrary")),
        name="experts",
    )(tile_expert, tile_valid, *([xs] * MOE_KQ), w_gate_up, w_gate_up, b_gu, b_gu,
      w_down, b_down.reshape(N_EXPERTS, 1, D_MODEL), slot_gate)


def _dispatch(top_e, gate):
    n = top_e.shape[0]
    n_assign = n * TOP_K
    n_tiles = -(-n_assign // MOE_TM) + N_EXPERTS
    n_slots = n_tiles * MOE_TM
    flat_e = top_e.reshape(-1)
    onehot = (flat_e[:, None] == jnp.arange(N_EXPERTS, dtype=jnp.int32)[None, :]).astype(jnp.int32)
    csum = jnp.cumsum(onehot, axis=0)
    rank = jnp.sum(csum * onehot, axis=1) - 1
    counts = csum[-1]
    tiles = (counts + MOE_TM - 1) // MOE_TM
    tile_end = jnp.cumsum(tiles)
    pad_start = (tile_end - tiles) * MOE_TM
    dest = pad_start[flat_e] + rank
    tok = jnp.arange(n_assign, dtype=jnp.int32) // TOP_K
    gate_bits = lax.bitcast_convert_type(gate.reshape(-1), jnp.int32)
    pad_tok = jnp.arange(n_slots, dtype=jnp.int32) % n
    slots = jnp.stack([pad_tok, jnp.zeros((n_slots,), jnp.int32)], axis=1)
    slots = slots.at[dest].set(jnp.stack([tok, gate_bits], axis=1))
    slot_tok = slots[:, 0]
    slot_gate = lax.bitcast_convert_type(slots[:, 1], F32)
    tile_ids = jnp.arange(n_tiles, dtype=jnp.int32)
    tile_valid = (tile_ids < tile_end[-1]).astype(jnp.int32)
    last = jnp.maximum(tile_end[-1] - 1, 0)
    tile_expert = jnp.sum(
        (tile_end[None, :] <= jnp.minimum(tile_ids, last)[:, None]).astype(jnp.int32), axis=1)
    tile_expert = jnp.minimum(tile_expert, N_EXPERTS - 1).astype(jnp.int32)
    return slot_tok, slot_gate.reshape(n_slots, 1), tile_expert, tile_valid, dest.reshape(n, TOP_K)


def _final_kernel(h_ref, *refs):
    y_refs, g_ref, o_ref = refs[:TOP_K], refs[TOP_K], refs[TOP_K + 1]
    h = h_ref[...]
    for y_ref in y_refs:
        h = h + y_ref[...]
    o_ref[...] = _rms(h) * g_ref[...]


def _final(h, y_rows, g, row0, n_rows, tm):
    n = h.shape[0]
    blk0 = row0 // tm
    y_specs = [pl.BlockSpec((tm, D_MODEL), lambda i, k=k: (k * (n // tm) + blk0 + i, 0))
               for k in range(TOP_K)]
    return pl.pallas_call(
        _final_kernel,
        grid=(n_rows // tm,),
        in_specs=[pl.BlockSpec((tm, D_MODEL), lambda i: (blk0 + i, 0))] + y_specs
                 + [pl.BlockSpec((1, D_MODEL), lambda i: (0, 0))],
        out_specs=pl.BlockSpec((tm, D_MODEL), lambda i: (i, 0)),
        out_shape=jax.ShapeDtypeStruct((n_rows, D_MODEL), F32),
        compiler_params=_params(("parallel",)),
        name="final_norm",
    )(h, *([y_rows] * TOP_K), g.reshape(1, D_MODEL))


def kernel(x_prompt, x_sample, cache_k, cache_v, state_conv, norm_mix, w_in, conv_w,
           norm_attn_out, norm_conv_out, w_out, norm_ffn, w_router, b_router,
           w_gate_up, b_gate_up, w_down, b_down, norm_final):
    batch, seq, _ = x_prompt.shape
    bsz, t_new, _ = x_sample.shape
    depth = w_in.shape[0]
    assert depth == 1 and t_new >= CONV_WIDTH - 1
    n_p, n_s = batch * seq, bsz * t_new
    n = n_p + n_s
    layer = 0

    x_all = jnp.concatenate([x_prompt.reshape(n_p, D_MODEL), x_sample.reshape(n_s, D_MODEL)], axis=0)
    xn = _norm_cast(x_all, norm_mix[layer], 512)
    proj = _matmul(xn, w_in[layer].astype(BF16), None, 512, 1024)
    proj_s = lax.slice(proj, (n_p, 0), (n, D_IN))
    proj_s4 = proj_s.reshape(bsz, t_new, D_IN // HEAD_DIM, HEAD_DIM)

    attn_p = _prompt_attention(proj, batch, seq)
    attn_s = _sample_attention(proj_s4, cache_k[layer], cache_v[layer])
    conv_p, u_last_p = _prompt_conv(proj, conv_w[layer], batch, seq)
    conv_s, u_s = _sample_conv(proj_s[:, 3 * D_ATT:].reshape(bsz, t_new * 3 * D_CONV),
                               state_conv[layer].reshape(bsz, (CONV_WIDTH - 1) * D_CONV),
                               conv_w[layer])

    attn = jnp.concatenate([attn_p, attn_s.reshape(n_s, D_ATT)], axis=0)
    conv = jnp.concatenate([conv_p, conv_s.reshape(n_s, D_CONV)], axis=0)
    cat = _mixer_norm(attn, conv, norm_attn_out[layer], norm_conv_out[layer], 512)
    h1 = _matmul(cat, w_out[layer].astype(BF16), x_all, 512, 1024)

    xn2, top_e, gate = _router(h1, norm_ffn[layer], w_router[layer], b_router[layer], 512)
    slot_tok, slot_gate, tile_expert, tile_valid, dest = _dispatch(top_e, gate)
    xs = xn2[slot_tok]
    ys = _experts(xs, slot_gate, tile_expert, tile_valid, w_gate_up[layer], b_gate_up[layer],
                  w_down[layer], b_down[layer])
    y_rows = ys[dest.T.reshape(-1)]
    y_prompt = _final(h1, y_rows, norm_final, 0, n_p, 128).reshape(batch, seq, D_MODEL)
    y_sample = _final(h1, y_rows, norm_final, n_p, n_s, 128).reshape(bsz, t_new, D_MODEL)

    n_keep = min(MAX_WINDOW, seq)

    def prompt_window(col0):
        parts = [lax.slice(proj, ((b + 1) * seq - n_keep, col0), ((b + 1) * seq, col0 + D_ATT))
                 for b in range(batch)]
        return jnp.stack(parts).reshape(1, batch, n_keep, N_HEADS, HEAD_DIM)

    k_prompt_win = prompt_window(D_ATT)
    v_prompt_win = prompt_window(2 * D_ATT)
    conv_prompt = u_last_p[None]
    k_sample_new = proj_s4[:, :, N_HEADS:2 * N_HEADS][None]
    v_sample_new = proj_s4[:, :, 2 * N_HEADS:3 * N_HEADS][None]
    conv_sample = u_s.reshape(bsz, t_new, D_CONV)[:, t_new - (CONV_WIDTH - 1):][None]
    return (y_prompt, y_sample, k_prompt_win, v_prompt_win, conv_prompt,
            k_sample_new, v_sample_new, conv_sample)
```
